```python
import math
import jax, jax.numpy as jnp
from jax import lax
import numpy as np

D_MODEL = 2048
BATCH = 4
SEQ = 4096
DEPTH = 2
DEC_BATCH = 2
DEC_SEQ = 4096
PAST_LEN = 128

MIX_WIDTH = D_MODEL
A_HEADS = 6
A_HEAD_DIM = 128
A_WIDTH = A_HEADS * A_HEAD_DIM
CHUNK = 128
B_HEADS = 4
B_HEAD_DIM = 128
B_WIDTH = B_HEADS * B_HEAD_DIM
C_WIDTH = MIX_WIDTH - A_WIDTH - B_WIDTH
C_GROUP = 16
C_GROUPS = C_WIDTH // C_GROUP
C_STATE = 64
IN_WIDTH = 2 * A_WIDTH + B_WIDTH + C_WIDTH
D_FF = -(-8 * D_MODEL // (3 * 256)) * 256
EPS = 1e-6

kernel_name = "hybrid_gmlp_fnet_s5_encoder"


def rms_norm(x, g):
    xf = x.astype(jnp.float32)
    y = xf * lax.rsqrt(jnp.mean(xf * xf, axis=-1, keepdims=True) + EPS)
    return (y * g.astype(jnp.float32)).astype(x.dtype)


def gmlp_mixer(z, v_gain, w_s, b_s):
    nb, s, _ = z.shape
    z = jax.nn.gelu(z)
    u, v = jnp.split(z, 2, axis=-1)
    v = rms_norm(v.reshape(nb, s, A_HEADS, A_HEAD_DIM), v_gain.reshape(A_HEADS, A_HEAD_DIM))
    vc = v.reshape(nb, s // CHUNK, CHUNK, A_HEADS, A_HEAD_DIM)
    mixed = jnp.einsum('hqk,bnkhd->bnqhd', w_s, vc) + b_s.T[None, None, :, :, None]
    return u * mixed.reshape(nb, s, A_WIDTH)


def fourier_mixer(z):
    nb, s, _ = z.shape
    zh = z.reshape(nb, s, B_HEADS, B_HEAD_DIM).astype(jnp.float32)
    f = jnp.fft.fft2(zh, axes=(1, 3), norm='ortho').real
    return f.reshape(nb, s, B_WIDTH).astype(z.dtype)


def _complex_diag_scan(a_re, a_im, x_re, x_im, reverse):
    def combine(e1, e2):
        a1r, a1i, x1r, x1i = e1
        a2r, a2i, x2r, x2i = e2
        return (a2r * a1r - a2i * a1i,
                a2r * a1i + a2i * a1r,
                a2r * x1r - a2i * x1i + x2r,
                a2r * x1i + a2i * x1r + x2i)
    ar = jnp.broadcast_to(a_re, x_re.shape)
    ai = jnp.broadcast_to(a_im, x_re.shape)
    _, _, hr, hi = lax.associative_scan(combine, (ar, ai, x_re, x_im), axis=1, reverse=reverse)
    return hr, hi


def s5_mixer(z, lam_re, lam_im, log_step, b_re, b_im, c_re, c_im, d_skip, w_glu, b_glu):
    nb, s, _ = z.shape
    f32 = jnp.float32
    u = z.reshape(nb, s, C_GROUPS, C_GROUP).astype(f32)
    lr = lam_re.astype(f32)
    li = lam_im.astype(f32)
    step = jnp.exp(log_step.astype(f32))[..., None]
    mag = jnp.exp(lr * step)
    ab_re = mag * jnp.cos(li * step)
    ab_im = mag * jnp.sin(li * step)
    den = lr * lr + li * li
    nr = ab_re - 1.0
    q_re = (nr * lr + ab_im * li) / den
    q_im = (ab_im * lr - nr * li) / den
    br = b_re.astype(f32)[None]
    bi = b_im.astype(f32)[None]
    bb_re = q_re[..., None] * br - q_im[..., None] * bi
    bb_im = q_re[..., None] * bi + q_im[..., None] * br
    h_re = 0.0
    h_im = 0.0
    for t, rev in ((0, False), (1, True)):
        xr = jnp.einsum('bsgc,gpc->bsgp', u, bb_re[t])
        xi = jnp.einsum('bsgc,gpc->bsgp', u, bb_im[t])
        hr, hi = _complex_diag_scan(ab_re[t], ab_im[t], xr, xi, rev)
        h_re = h_re + hr
        h_im = h_im + hi
    y = (jnp.einsum('bsgp,gcp->bsgc', h_re, c_re.astype(f32))
         - jnp.einsum('bsgp,gcp->bsgc', h_im, c_im.astype(f32))
         + d_skip.astype(f32).reshape(C_GROUPS, C_GROUP) * u)
    y = jax.nn.gelu(y.reshape(nb, s, C_WIDTH)).astype(z.dtype)
    return y * jax.nn.sigmoid(y @ w_glu + b_glu)


def trunk(x, p):
    for l in range(DEPTH):
        h = rms_norm(x, p['norm1_g'][l])
        zin = h @ p['w_in'][l]
        za = zin[..., :2 * A_WIDTH]
        zb = zin[..., 2 * A_WIDTH:2 * A_WIDTH + B_WIDTH]
        zc = zin[..., 2 * A_WIDTH + B_WIDTH:]
        ya = gmlp_mixer(za, p['a_v_g'][l], p['a_ws'][l], p['a_bs'][l])
        yb = fourier_mixer(zb)
        yc = s5_mixer(zc, p['c_lam_re'][l], p['c_lam_im'][l], p['c_log_step'][l],
                      p['c_b_re'][l], p['c_b_im'][l], p['c_c_re'][l], p['c_c_im'][l],
                      p['c_d'][l], p['c_w_glu'][l], p['c_b_glu'][l])
        og = p['out_norm_g'][l]
        y = jnp.concatenate([
            rms_norm(ya, og[:A_WIDTH]),
            rms_norm(yb, og[A_WIDTH:A_WIDTH + B_WIDTH]),
            rms_norm(yc, og[A_WIDTH + B_WIDTH:])], axis=-1)
        x = x + y @ p['w_out'][l]
        h = rms_norm(x, p['norm2_g'][l])
        x = x + (jax.nn.silu(h @ p['w_gate'][l]) * (h @ p['w_up'][l])) @ p['w_down'][l]
    return rms_norm(x, p['final_g'])


def setup_inputs(seed: int = 0) -> dict:
    key = jax.random.key(seed)
    k = jax.random.split(key, 24)
    f32 = jnp.float32
    nrm = lambda kk, shape, scale: jax.random.normal(kk, shape, f32) * scale
    n_idx = jnp.arange(C_STATE, dtype=f32)
    return {
        'x_prompt': nrm(k[0], (BATCH, SEQ, D_MODEL), 1.0),
        'x_sample': nrm(k[1], (DEC_BATCH, DEC_SEQ, D_MODEL), 1.0),
        'norm1_g': 1.0 + nrm(k[2], (DEPTH, D_MODEL), 0.02),
        'w_in': nrm(k[3], (DEPTH, D_MODEL, IN_WIDTH), D_MODEL ** -0.5),
        'a_v_g': 1.0 + nrm(k[4], (DEPTH, A_WIDTH), 0.02),
        'a_ws': nrm(k[5], (DEPTH, A_HEADS, CHUNK, CHUNK), 0.5 * CHUNK ** -0.5),
        'a_bs': 1.0 + nrm(k[6], (DEPTH, A_HEADS, CHUNK), 0.02),
        'c_lam_re': -0.5 + nrm(k[7], (DEPTH, 2, C_GROUPS, C_STATE), 0.01),
        'c_lam_im': math.pi * n_idx + nrm(k[8], (DEPTH, 2, C_GROUPS, C_STATE), 0.01),
        'c_log_step': jax.random.uniform(k[9], (DEPTH, 2, C_GROUPS), f32, math.log(1e-3), math.log(1e-1)),
        'c_b_re': nrm(k[10], (DEPTH, C_GROUPS, C_STATE, C_GROUP), (2 * C_GROUP) ** -0.5),
        'c_b_im': nrm(k[11], (DEPTH, C_GROUPS, C_STATE, C_GROUP), (2 * C_GROUP) ** -0.5),
        'c_c_re': nrm(k[12], (DEPTH, C_GROUPS, C_GROUP, C_STATE), C_STATE ** -0.5),
        'c_c_im': nrm(k[13], (DEPTH, C_GROUPS, C_GROUP, C_STATE), C_STATE ** -0.5),
        'c_d': nrm(k[14], (DEPTH, C_WIDTH), 1.0),
        'c_w_glu': nrm(k[15], (DEPTH, C_WIDTH, C_WIDTH), C_WIDTH ** -0.5),
        'c_b_glu': nrm(k[16], (DEPTH, C_WIDTH), 0.02),
        'out_norm_g': 1.0 + nrm(k[17], (DEPTH, MIX_WIDTH), 0.02),
        'w_out': nrm(k[18], (DEPTH, MIX_WIDTH, D_MODEL), MIX_WIDTH ** -0.5),
        'norm2_g': 1.0 + nrm(k[19], (DEPTH, D_MODEL), 0.02),
        'w_gate': nrm(k[20], (DEPTH, D_MODEL, D_FF), D_MODEL ** -0.5),
        'w_up': nrm(k[21], (DEPTH, D_MODEL, D_FF), D_MODEL ** -0.5),
        'w_down': nrm(k[22], (DEPTH, D_FF, D_MODEL), D_FF ** -0.5),
        'final_g': 1.0 + nrm(k[23], (D_MODEL,), 0.02),
    }


def reference(x_prompt, x_sample, norm1_g, w_in, a_v_g, a_ws, a_bs, c_lam_re, c_lam_im,
              c_log_step, c_b_re, c_b_im, c_c_re, c_c_im, c_d, c_w_glu, c_b_glu,
              out_norm_g, w_out, norm2_g, w_gate, w_up, w_down, final_g):
    p = dict(norm1_g=norm1_g, w_in=w_in, a_v_g=a_v_g, a_ws=a_ws, a_bs=a_bs,
             c_lam_re=c_lam_re, c_lam_im=c_lam_im, c_log_step=c_log_step,
             c_b_re=c_b_re, c_b_im=c_b_im, c_c_re=c_c_re, c_c_im=c_c_im, c_d=c_d,
             c_w_glu=c_w_glu, c_b_glu=c_b_glu, out_norm_g=out_norm_g, w_out=w_out,
             norm2_g=norm2_g, w_gate=w_gate, w_up=w_up, w_down=w_down, final_g=final_g)
    y_prompt = trunk(x_prompt, p)
    y_sample = trunk(x_sample, p)
    return (y_prompt, y_sample)
```

```python
import functools
import math

import jax
import jax.numpy as jnp
from jax import lax
from jax.experimental import pallas as pl
from jax.experimental.pallas import tpu as pltpu

F32 = jnp.float32
BF16 = jnp.bfloat16

EPS = 1e-6
A_HEADS = 6
A_HEAD_DIM = 128
A_WIDTH = A_HEADS * A_HEAD_DIM
CHUNK = 128
B_HEADS = 4
B_HEAD_DIM = 128
B_WIDTH = B_HEADS * B_HEAD_DIM
C_GROUP = 16
C_STATE = 64

LANES = 128
SUBLANES = 8
S5_CHUNK = 16
S5_COLS = S5_CHUNK * C_GROUP
VMEM_LIMIT_BYTES = 56 * 1024 * 1024

TOKEN_TILE = 512
FF_TILE = 512
DFT_ROW_TILE = 512
GELU_C = math.sqrt(2.0 / math.pi)


def _gelu(x):
    return 0.5 * x * (1.0 + jnp.tanh(GELU_C * (x + 0.044715 * (x * x * x))))


def _sigmoid(x):
    return 1.0 / (1.0 + jnp.exp(-x))


def _rms(x, g):
    return x * lax.rsqrt(jnp.mean(x * x, axis=-1, keepdims=True) + EPS) * g


def _dot(a, b):
    return jnp.dot(a, b, preferred_element_type=F32)


def _const_spec(shape):
    nd = len(shape)
    return pl.BlockSpec(shape, lambda *_: (0,) * nd, pipeline_mode=pl.Buffered(1))


def _token_tile(t):
    tm = min(TOKEN_TILE, t)
    assert t % tm == 0 and tm % CHUNK == 0, (t, tm)
    return tm


def _params(*sem):
    return pltpu.CompilerParams(dimension_semantics=sem, vmem_limit_bytes=VMEM_LIMIT_BYTES)


def _in_body(x_ref, g1_ref, w_ref, vg_ref, ws_ref, bs_ref, oga_ref, cd_ref, sd_ref,
             ya_ref, xc_ref, xs_ref, zc_ref, ya_scr):
    tm = x_ref.shape[0]
    h = _rms(x_ref[...], g1_ref[...]).astype(BF16)
    for hd in range(A_HEADS):
        lo = hd * A_HEAD_DIM
        u = _gelu(_dot(h, w_ref[:, lo:lo + A_HEAD_DIM]))
        v = _gelu(_dot(h, w_ref[:, A_WIDTH + lo:A_WIDTH + lo + A_HEAD_DIM]))
        v = _rms(v, vg_ref[:, lo:lo + A_HEAD_DIM]).astype(BF16)
        for c in range(tm // CHUNK):
            r = c * CHUNK
            mixed = _dot(ws_ref[hd], v[r:r + CHUNK, :]) + bs_ref[hd]
            ya_scr[r:r + CHUNK, lo:lo + A_HEAD_DIM] = u[r:r + CHUNK, :] * mixed
    ya_ref[...] = _rms(ya_scr[...], oga_ref[...]).astype(BF16)
    for g in range(B_HEADS):
        lo = 2 * A_WIDTH + g * B_HEAD_DIM
        zb = _dot(h, w_ref[:, lo:lo + B_HEAD_DIM]).astype(BF16)
        xc_ref[:, g * B_HEAD_DIM:(g + 1) * B_HEAD_DIM] = _dot(zb, cd_ref[...]).astype(BF16)
        xs_ref[:, g * B_HEAD_DIM:(g + 1) * B_HEAD_DIM] = _dot(zb, sd_ref[...]).astype(BF16)
    zc_ref[...] = _dot(h, w_ref[:, 2 * A_WIDTH + B_WIDTH:]).astype(BF16)


def _in_call(x, g1, w_in, vg, ws, bs, oga, cd, sd):
    t, d = x.shape
    n_in = w_in.shape[1]
    c_width = n_in - 2 * A_WIDTH - B_WIDTH
    tm = _token_tile(t)
    row = lambda w: pl.BlockSpec((tm, w), lambda i: (i, 0))
    return pl.pallas_call(
        _in_body,
        grid=(t // tm,),
        in_specs=[row(d), _const_spec((1, d)), _const_spec((d, n_in)), _const_spec((1, A_WIDTH)),
                  _const_spec(ws.shape), _const_spec(bs.shape), _const_spec((1, A_WIDTH)),
                  _const_spec(cd.shape), _const_spec(sd.shape)],
        out_specs=[row(A_WIDTH), row(B_WIDTH), row(B_WIDTH), row(c_width)],
        out_shape=[jax.ShapeDtypeStruct((t, A_WIDTH), BF16), jax.ShapeDtypeStruct((t, B_WIDTH), BF16),
                   jax.ShapeDtypeStruct((t, B_WIDTH), BF16), jax.ShapeDtypeStruct((t, c_width), BF16)],
        scratch_shapes=[pltpu.VMEM((tm, A_WIDTH), F32)],
        compiler_params=_params("parallel"),
        name="in_proj_gmlp",
    )(x, g1, w_in, vg, ws, bs, oga, cd, sd)


def _dft_body(cs_ref, ssn_ref, xc_ref, xs_ref, og_ref, o_ref, *, scale):
    y = _dot(cs_ref[...], xc_ref[...]) + _dot(ssn_ref[...], xs_ref[...])
    o_ref[...] = _rms(y * scale, og_ref[...]).astype(BF16)


def _dft_call(cs, ssn, xc, xs, ogb, seq):
    t, w = xc.shape
    nb = t // seq
    tk = min(DFT_ROW_TILE, seq)
    nk = seq // tk
    scale = 1.0 / math.sqrt(seq * B_HEAD_DIM)
    lhs = pl.BlockSpec((tk, seq), lambda k, b: (k, 0))
    rhs = pl.BlockSpec((seq, w), lambda k, b: (b, 0))
    return pl.pallas_call(
        functools.partial(_dft_body, scale=scale),
        grid=(nk, nb),
        in_specs=[lhs, lhs, rhs, rhs, pl.BlockSpec((1, w), lambda k, b: (0, 0))],
        out_specs=pl.BlockSpec((tk, w), lambda k, b: (b * nk + k, 0)),
        out_shape=jax.ShapeDtypeStruct((t, w), BF16),
        compiler_params=_params("parallel", "parallel"),
        name="fnet_seq_dft",
    )(cs, ssn, xc, xs, ogb)


def _s5_body(u_ref, p_ref, w_ref, are_ref, aim_ref, y_ref, s_scr, ha_scr, hb_scr, *, rows):
    n = u_ref.shape[1]
    nchunks = n // rows
    u = u_ref[0]
    s_scr[...] = _dot(u, p_ref[0])
    ar = jnp.broadcast_to(are_ref[0], (rows, LANES))
    ai = jnp.broadcast_to(aim_ref[0], (rows, LANES))
    is_fwd = lax.broadcasted_iota(jnp.int32, (rows, LANES), 1) < C_STATE
    ha_scr[0:rows, :] = jnp.zeros((rows, 2 * LANES), F32)
    hb_scr[n - rows:n, :] = jnp.zeros((rows, 2 * LANES), F32)

    def step(c, carry):
        hre, him = carry
        r_f = pl.multiple_of(c * rows, rows)
        r_b = pl.multiple_of((nchunks - 1 - c) * rows, rows)
        sa = s_scr[pl.ds(r_f, rows), :]
        sb = s_scr[pl.ds(r_b, rows), :]
        sre = jnp.where(is_fwd, sa[:, :LANES], sb[:, :LANES])
        sim = jnp.where(is_fwd, sa[:, LANES:], sb[:, LANES:])
        nre = hre * ar - him * ai + sre
        nim = hre * ai + him * ar + sim
        hcat = jnp.concatenate([nre, nim], axis=1)
        ha_scr[pl.ds(pl.multiple_of(r_f + rows, rows), rows), :] = hcat
        hb_scr[pl.ds(pl.multiple_of(r_b - rows, rows), rows), :] = hcat
        return nre, nim

    zero = jnp.zeros((rows, LANES), F32)
    lax.fori_loop(0, nchunks - 1, step, (zero, zero))
    y = (_dot(u, w_ref[0, 0:S5_COLS, :])
         + _dot(ha_scr[...].astype(BF16), w_ref[0, S5_COLS:2 * S5_COLS, :])
         + _dot(hb_scr[...].astype(BF16), w_ref[0, 2 * S5_COLS:3 * S5_COLS, :]))
    y_ref[0] = y.astype(BF16)


def _s5_call(uc, p, w, are, aim, rows):
    g, n, cols = uc.shape
    blk = lambda a: pl.BlockSpec((1,) + a.shape[1:], lambda i: (i, 0, 0))
    return pl.pallas_call(
        functools.partial(_s5_body, rows=rows),
        grid=(g,),
        in_specs=[blk(uc), blk(p), blk(w), blk(are), blk(aim)],
        out_specs=pl.BlockSpec((1, n, cols), lambda i: (i, 0, 0)),
        out_shape=jax.ShapeDtypeStruct((g, n, cols), BF16),
        scratch_shapes=[pltpu.VMEM((n, 2 * LANES), F32)] * 3,
        compiler_params=_params("parallel"),
        name="s5_chunk_scan",
    )(uc, p, w, are, aim)


def _s5_weights(lam_re, lam_im, log_step, b_re, b_im, c_re, c_im, d_skip):
    L = S5_CHUNK
    G, P = lam_re.shape[1:]
    step = jnp.exp(log_step)[..., None]
    lr, li = lam_re, lam_im
    mag = jnp.exp(lr * step)
    ab_re, ab_im = mag * jnp.cos(li * step), mag * jnp.sin(li * step)
    den = lr * lr + li * li
    nr = ab_re - 1.0
    q_re = (nr * lr + ab_im * li) / den
    q_im = (ab_im * lr - nr * li) / den
    bb_re = q_re[..., None] * b_re[None] - q_im[..., None] * b_im[None]
    bb_im = q_re[..., None] * b_im[None] + q_im[..., None] * b_re[None]
    k = jnp.arange(L + 1, dtype=F32)[:, None, None, None]
    pmag = jnp.exp(k * (lr * step))
    pw_re, pw_im = pmag * jnp.cos(k * (li * step)), pmag * jnp.sin(k * (li * step))
    w_re = pw_re[..., None] * bb_re - pw_im[..., None] * bb_im
    w_im = pw_re[..., None] * bb_im + pw_im[..., None] * bb_re
    kern = (jnp.einsum('gop,ktgpc->ktgoc', c_re, w_re[:L]) - jnp.einsum('gop,ktgpc->ktgoc', c_im, w_im[:L]))
    i = jnp.arange(L)[:, None]
    j = jnp.arange(L)[None, :]
    kf = jnp.where((j >= i)[..., None, None, None], kern[:, 0][jnp.clip(j - i, 0, L - 1)], 0.0)
    kb = jnp.where((i >= j)[..., None, None, None], kern[:, 1][jnp.clip(i - j, 0, L - 1)], 0.0)
    dmat = (jnp.eye(L)[:, :, None, None, None] * jnp.eye(C_GROUP)[None, None, None]
            * d_skip.reshape(G, C_GROUP)[None, None, :, None, :])
    m = (kf + kb + dmat).transpose(2, 0, 4, 1, 3).reshape(G, S5_COLS, S5_COLS)
    to_rows = lambda a: a.transpose(1, 0, 3, 2).reshape(G, S5_COLS, P)
    p_mat = jnp.concatenate([to_rows(w_re[:L, 0][::-1]), to_rows(w_re[:L, 1]),
                             to_rows(w_im[:L, 0][::-1]), to_rows(w_im[:L, 1])], axis=-1)
    def readout(pr, pi):
        wr = c_re[None] * pr[:, :, None, :] - c_im[None] * pi[:, :, None, :]
        wi = c_re[None] * pi[:, :, None, :] + c_im[None] * pr[:, :, None, :]
        fl = lambda a: a.transpose(1, 3, 0, 2).reshape(G, P, S5_COLS)
        return fl(wr), fl(-wi)
    qf_re, qf_im = readout(pw_re[1:, 0], pw_im[1:, 0])
    qb_re, qb_im = readout(pw_re[1:, 1][::-1], pw_im[1:, 1][::-1])
    z = jnp.zeros_like(qf_re)
    qf = jnp.concatenate([qf_re, z, qf_im, z], axis=1)
    qb = jnp.concatenate([z, qb_re, z, qb_im], axis=1)
    w_all = jnp.concatenate([m, qf, qb], axis=1)
    a_re = jnp.concatenate([pw_re[L, 0], pw_re[L, 1]], axis=-1)[:, None, :]
    a_im = jnp.concatenate([pw_im[L, 0], pw_im[L, 1]], axis=-1)[:, None, :]
    return p_mat.astype(BF16), w_all.astype(BF16), a_re, a_im


def _s5_mixer(zc, nb, seq, weights):
    p_mat, w_all, a_re, a_im = weights
    G = p_mat.shape[0]
    nchunks = seq // S5_CHUNK
    rows = -(-nb // SUBLANES) * SUBLANES
    u = zc.reshape(nb, nchunks, S5_CHUNK, G, C_GROUP).transpose(3, 1, 0, 2, 4)
    u = jnp.pad(u, ((0, 0), (0, 0), (0, rows - nb), (0, 0), (0, 0))).reshape(G, nchunks * rows, S5_COLS)
    y = _s5_call(u, p_mat, w_all, a_re, a_im, rows)
    y = y.reshape(G, nchunks, rows, S5_CHUNK, C_GROUP)[:, :, :nb].transpose(2, 1, 3, 0, 4)
    return y.reshape(nb * seq, G * C_GROUP)


def _out_body(x_ref, ya_ref, yb_ref, yc_ref, wg_ref, bg_ref, ogc_ref, wo_ref, o_ref):
    y = _gelu(yc_ref[...].astype(F32))
    yc = y * _sigmoid(_dot(y.astype(BF16), wg_ref[...]) + bg_ref[...])
    yc = _rms(yc, ogc_ref[...]).astype(BF16)
    acc = x_ref[...] + _dot(ya_ref[...], wo_ref[0:A_WIDTH, :])
    acc = acc + _dot(yb_ref[...], wo_ref[A_WIDTH:A_WIDTH + B_WIDTH, :])
    o_ref[...] = acc + _dot(yc, wo_ref[A_WIDTH + B_WIDTH:, :])


def _out_call(x, ya, yb, yc, w_glu, b_glu, ogc, w_out):
    t, d = x.shape
    cw = yc.shape[1]
    tm = _token_tile(t)
    row = lambda w: pl.BlockSpec((tm, w), lambda i: (i, 0))
    return pl.pallas_call(
        _out_body,
        grid=(t // tm,),
        in_specs=[row(d), row(A_WIDTH), row(B_WIDTH), row(cw), _const_spec((cw, cw)), _const_spec((1, cw)),
                  _const_spec((1, cw)), _const_spec(w_out.shape)],
        out_specs=row(d),
        out_shape=jax.ShapeDtypeStruct((t, d), F32),
        compiler_params=_params("parallel"),
        name="glu_out_proj",
    )(x, ya, yb, yc, w_glu, b_glu, ogc, w_out)


def _ffn_body(x_ref, g2_ref, wg_ref, wu_ref, wd_ref, gf_ref, o_ref, h_scr, *, final_norm):
    f = pl.program_id(1)

    @pl.when(f == 0)
    def _():
        x = x_ref[...]
        h_scr[...] = _rms(x, g2_ref[...]).astype(BF16)
        o_ref[...] = x

    h = h_scr[...]
    gate = _dot(h, wg_ref[...])
    a = (gate * _sigmoid(gate) * _dot(h, wu_ref[...])).astype(BF16)
    o_ref[...] += _dot(a, wd_ref[...])

    if final_norm:
        @pl.when(f == pl.num_programs(1) - 1)
        def _():
            o_ref[...] = _rms(o_ref[...], gf_ref[...])


def _ffn_call(x, g2, w_gate, w_up, w_down, gf, final_norm):
    t, d = x.shape
    ff = w_gate.shape[1]
    tm = _token_tile(t)
    tf = FF_TILE
    return pl.pallas_call(
        functools.partial(_ffn_body, final_norm=final_norm),
        grid=(t // tm, ff // tf),
        in_specs=[pl.BlockSpec((tm, d), lambda i, f: (i, 0)), _const_spec((1, d)),
                  pl.BlockSpec((d, tf), lambda i, f: (0, f)), pl.BlockSpec((d, tf), lambda i, f: (0, f)),
                  pl.BlockSpec((tf, d), lambda i, f: (f, 0)), _const_spec((1, d))],
        out_specs=pl.BlockSpec((tm, d), lambda i, f: (i, 0)),
        out_shape=jax.ShapeDtypeStruct((t, d), F32),
        scratch_shapes=[pltpu.VMEM((tm, d), BF16)],
        compiler_params=_params("parallel", "arbitrary"),
        name="swiglu_ffn",
    )(x, g2, w_gate, w_up, w_down, gf)


def _dft_matrices(n):
    idx = (jnp.arange(n, dtype=jnp.int32)[:, None] * jnp.arange(n, dtype=jnp.int32)[None, :]) % n
    ang = idx.astype(F32) * (2.0 * math.pi / n)
    return jnp.cos(ang), -jnp.sin(ang)


def kernel(x_prompt, x_sample, norm1_g, w_in, a_v_g, a_ws, a_bs, c_lam_re, c_lam_im, c_log_step, c_b_re, c_b_im,
           c_c_re, c_c_im, c_d, c_w_glu, c_b_glu, out_norm_g, w_out, norm2_g, w_gate, w_up, w_down, final_g):
    seq, d = x_prompt.shape[1:]
    assert x_sample.shape[1:] == (seq, d)
    nb = x_prompt.shape[0] + x_sample.shape[0]
    depth = w_in.shape[0]
    x = jnp.concatenate([x_prompt.reshape(-1, d), x_sample.reshape(-1, d)], axis=0)

    cs, ssn = _dft_matrices(seq)
    cs, ssn = cs.astype(BF16), ssn.astype(BF16)
    cd, sdn = _dft_matrices(B_HEAD_DIM)
    cd, sd = cd.astype(BF16), (-sdn).astype(BF16)
    row = lambda v: v.reshape(1, -1)

    for l in range(depth):
        og = out_norm_g[l]
        ws = a_ws[l].astype(BF16)
        bs = jnp.broadcast_to(a_bs[l][:, :, None], (A_HEADS, CHUNK, A_HEAD_DIM))
        ya, xc, xs, zc = _in_call(x, row(norm1_g[l]), w_in[l].astype(BF16), row(a_v_g[l]), ws, bs,
                                  row(og[:A_WIDTH]), cd, sd)
        yb = _dft_call(cs, ssn, xc, xs, row(og[A_WIDTH:A_WIDTH + B_WIDTH]), seq)
        s5w = _s5_weights(c_lam_re[l], c_lam_im[l], c_log_step[l], c_b_re[l], c_b_im[l], c_c_re[l], c_c_im[l],
                          c_d[l])
        yc = _s5_mixer(zc, nb, seq, s5w)
        x = _out_call(x, ya, yb, yc, c_w_glu[l].astype(BF16), row(c_b_glu[l]), row(og[A_WIDTH + B_WIDTH:]),
                      w_out[l].astype(BF16))
        x = _ffn_call(x, row(norm2_g[l]), w_gate[l].astype(BF16), w_up[l].astype(BF16), w_down[l].astype(BF16),
                      row(final_g), final_norm=(l == depth - 1))

    n_prompt = x_prompt.shape[0] * seq
    return (x[:n_prompt].reshape(x_prompt.shape), x[n_prompt:].reshape(x_sample.shape))
```

```python
import functools
import math

import jax
import jax.numpy as jnp
from jax import lax
from jax.experimental import pallas as pl
from jax.experimental.pallas import tpu as pltpu

F32 = jnp.float32
BF16 = jnp.bfloat16

EPS = 1e-6
A_HEADS = 6
A_HEAD_DIM = 128
A_WIDTH = A_HEADS * A_HEAD_DIM
CHUNK = 128
B_HEADS = 4
B_HEAD_DIM = 128
B_WIDTH = B_HEADS * B_HEAD_DIM
C_GROUP = 16
C_STATE = 64

LANES = 128
SUBLANES = 8
S5_CHUNK = 16
S5_COLS = S5_CHUNK * C_GROUP
VMEM_LIMIT_BYTES = 56 * 1024 * 1024

TOKEN_TILE = 512
FF_TILE = 512
DFT_ROW_TILE = 512
GELU_C = math.sqrt(2.0 / math.pi)


def _gelu(x):
    return 0.5 * x * (1.0 + jnp.tanh(GELU_C * (x + 0.044715 * (x * x * x))))


def _sigmoid(x):
    return 1.0 / (1.0 + jnp.exp(-x))


def _rms(x, g):
    return x * lax.rsqrt(jnp.mean(x * x, axis=-1, keepdims=True) + EPS) * g


def _dot(a, b):
    return jnp.dot(a, b, preferred_element_type=F32)


def _const_spec(shape):
    nd = len(shape)
    return pl.BlockSpec(shape, lambda *_: (0,) * nd, pipeline_mode=pl.Buffered(1))


def _row_tile_specs(xs, tm):
    specs, bounds, start = [], [], 0
    for x in xs:
        n = x.shape[0] // tm
        assert n * tm == x.shape[0] and tm % CHUNK == 0, (x.shape, tm)
        specs.append(pl.BlockSpec((tm, x.shape[1]),
                                  lambda i, *_, start=start, n=n: (jnp.clip(i - start, 0, n - 1), 0)))
        start += n
        bounds.append(start)
    return specs, tuple(bounds)


def _select_rows(x_refs, bounds):
    x = x_refs[-1][...]
    for k in range(len(x_refs) - 2, -1, -1):
        x = jnp.where(pl.program_id(0) < bounds[k], x_refs[k][...], x)
    return x


def _params(*sem):
    return pltpu.CompilerParams(dimension_semantics=sem, vmem_limit_bytes=VMEM_LIMIT_BYTES)


def _in_body(*refs, tile_bounds):
    nx = len(tile_bounds)
    x_refs = refs[:nx]
    (g1_ref, w_ref, vg_ref, ws_ref, bs_ref, oga_ref, cds_ref,
     ya_ref, xc_ref, xs_ref, zc_ref, z_scr, ya_scr) = refs[nx:]
    tm = ya_ref.shape[0]
    nch = tm // CHUNK
    h = _rms(_select_rows(x_refs, tile_bounds), g1_ref[...]).astype(BF16)
    z_scr[...] = _dot(h, w_ref[:, 0:2 * A_WIDTH])
    for hd in range(A_HEADS):
        lo = hd * A_HEAD_DIM
        u = _gelu(z_scr[:, lo:lo + A_HEAD_DIM])
        v = _gelu(z_scr[:, A_WIDTH + lo:A_WIDTH + lo + A_HEAD_DIM])
        v = _rms(v, vg_ref[:, lo:lo + A_HEAD_DIM]).astype(BF16)
        v_wide = jnp.concatenate([v[c * CHUNK:(c + 1) * CHUNK, :] for c in range(nch)], axis=1)
        mixed = _dot(ws_ref[hd], v_wide)
        for c in range(nch):
            r = c * CHUNK
            ya_scr[r:r + CHUNK, lo:lo + A_HEAD_DIM] = u[r:r + CHUNK, :] * (
                mixed[:, c * A_HEAD_DIM:(c + 1) * A_HEAD_DIM] + bs_ref[hd])
    ya_ref[...] = _rms(ya_scr[...], oga_ref[...]).astype(BF16)
    zb = _dot(h, w_ref[:, 2 * A_WIDTH:2 * A_WIDTH + B_WIDTH]).astype(BF16)
    for g in range(B_HEADS):
        lo = g * B_HEAD_DIM
        t = _dot(zb[:, lo:lo + B_HEAD_DIM], cds_ref[...])
        xc_ref[:, lo:lo + B_HEAD_DIM] = t[:, :B_HEAD_DIM].astype(BF16)
        xs_ref[:, lo:lo + B_HEAD_DIM] = t[:, B_HEAD_DIM:].astype(BF16)
    zc_ref[...] = _dot(h, w_ref[:, 2 * A_WIDTH + B_WIDTH:]).astype(BF16)


def _in_call(xs, g1, w_in, vg, ws, bs, oga, cds):
    d = xs[0].shape[1]
    n_in = w_in.shape[1]
    c_width = n_in - 2 * A_WIDTH - B_WIDTH
    tm = TOKEN_TILE
    x_specs, tile_bounds = _row_tile_specs(xs, tm)
    t = tile_bounds[-1] * tm
    row = lambda w: pl.BlockSpec((tm, w), lambda i: (i, 0))
    return pl.pallas_call(
        functools.partial(_in_body, tile_bounds=tile_bounds),
        grid=(t // tm,),
        in_specs=x_specs + [_const_spec((1, d)), _const_spec((d, n_in)), _const_spec((1, A_WIDTH)),
                            _const_spec(ws.shape), _const_spec(bs.shape), _const_spec((1, A_WIDTH)),
                            _const_spec(cds.shape)],
        out_specs=[row(A_WIDTH), row(B_WIDTH), row(B_WIDTH), row(c_width)],
        out_shape=[jax.ShapeDtypeStruct((t, A_WIDTH), BF16), jax.ShapeDtypeStruct((t, B_WIDTH), BF16),
                   jax.ShapeDtypeStruct((t, B_WIDTH), BF16), jax.ShapeDtypeStruct((t, c_width), BF16)],
        scratch_shapes=[pltpu.VMEM((tm, 2 * A_WIDTH), F32), pltpu.VMEM((tm, A_WIDTH), F32)],
        compiler_params=_params("parallel"),
        name="in_proj_gmlp",
    )(*xs, g1, w_in, vg, ws, bs, oga, cds)


def _dft_body(cs_ref, ssn_ref, xc_ref, xs_ref, og_ref, o_ref, *, scale):
    y = _dot(cs_ref[...], xc_ref[...]) + _dot(ssn_ref[...], xs_ref[...])
    o_ref[...] = _rms(y * scale, og_ref[...]).astype(BF16)


def _dft_call(cs, ssn, xc, xs, ogb, seq):
    t, w = xc.shape
    nb = t // seq
    tk = min(DFT_ROW_TILE, seq)
    nk = seq // tk
    scale = 1.0 / math.sqrt(seq * B_HEAD_DIM)
    lhs = pl.BlockSpec((tk, seq), lambda k, b: (k, 0))
    rhs = pl.BlockSpec((seq, w), lambda k, b: (b, 0))
    return pl.pallas_call(
        functools.partial(_dft_body, scale=scale),
        grid=(nk, nb),
        in_specs=[lhs, lhs, rhs, rhs, pl.BlockSpec((1, w), lambda k, b: (0, 0))],
        out_specs=pl.BlockSpec((tk, w), lambda k, b: (b * nk + k, 0)),
        out_shape=jax.ShapeDtypeStruct((t, w), BF16),
        compiler_params=_params("parallel", "parallel"),
        name="fnet_seq_dft",
    )(cs, ssn, xc, xs, ogb)


def _s5_body(u_ref, p_ref, w_ref, are_ref, aim_ref, y_ref, s_scr, ha_scr, hb_scr, *, rows):
    n = u_ref.shape[1]
    nchunks = n // rows
    u = u_ref[0]
    s_scr[...] = _dot(u, p_ref[0])
    ar = jnp.broadcast_to(are_ref[0], (rows, LANES))
    ai = jnp.broadcast_to(aim_ref[0], (rows, LANES))
    is_fwd = lax.broadcasted_iota(jnp.int32, (rows, LANES), 1) < C_STATE
    ha_scr[0:rows, :] = jnp.zeros((rows, 2 * LANES), F32)
    hb_scr[n - rows:n, :] = jnp.zeros((rows, 2 * LANES), F32)

    def step(c, carry):
        hre, him = carry
        r_f = pl.multiple_of(c * rows, rows)
        r_b = pl.multiple_of((nchunks - 1 - c) * rows, rows)
        sa = s_scr[pl.ds(r_f, rows), :]
        sb = s_scr[pl.ds(r_b, rows), :]
        sre = jnp.where(is_fwd, sa[:, :LANES], sb[:, :LANES])
        sim = jnp.where(is_fwd, sa[:, LANES:], sb[:, LANES:])
        nre = hre * ar - him * ai + sre
        nim = hre * ai + him * ar + sim
        hcat = jnp.concatenate([nre, nim], axis=1)
        ha_scr[pl.ds(pl.multiple_of(r_f + rows, rows), rows), :] = hcat
        hb_scr[pl.ds(pl.multiple_of(r_b - rows, rows), rows), :] = hcat
        return nre, nim

    zero = jnp.zeros((rows, LANES), F32)
    lax.fori_loop(0, nchunks - 1, step, (zero, zero))
    y = (_dot(u, w_ref[0, 0:S5_COLS, :])
         + _dot(ha_scr[...].astype(BF16), w_ref[0, S5_COLS:2 * S5_COLS, :])
         + _dot(hb_scr[...].astype(BF16), w_ref[0, 2 * S5_COLS:3 * S5_COLS, :]))
    y_ref[0] = y.astype(BF16)


def _s5_call(uc, p, w, are, aim, rows):
    g, n, cols = uc.shape
    blk = lambda a: pl.BlockSpec((1,) + a.shape[1:], lambda i: (i, 0, 0))
    return pl.pallas_call(
        functools.partial(_s5_body, rows=rows),
        grid=(g,),
        in_specs=[blk(uc), blk(p), blk(w), blk(are), blk(aim)],
        out_specs=pl.BlockSpec((1, n, cols), lambda i: (i, 0, 0)),
        out_shape=jax.ShapeDtypeStruct((g, n, cols), BF16),
        scratch_shapes=[pltpu.VMEM((n, 2 * LANES), F32)] * 3,
        compiler_params=_params("parallel"),
        name="s5_chunk_scan",
    )(uc, p, w, are, aim)


def _s5_weights(lam_re, lam_im, log_step, b_re, b_im, c_re, c_im, d_skip):
    L = S5_CHUNK
    G, P = lam_re.shape[1:]
    step = jnp.exp(log_step)[..., None]
    lr, li = lam_re, lam_im
    mag = jnp.exp(lr * step)
    ab_re, ab_im = mag * jnp.cos(li * step), mag * jnp.sin(li * step)
    den = lr * lr + li * li
    nr = ab_re - 1.0
    q_re = (nr * lr + ab_im * li) / den
    q_im = (ab_im * lr - nr * li) / den
    bb_re = q_re[..., None] * b_re[None] - q_im[..., None] * b_im[None]
    bb_im = q_re[..., None] * b_im[None] + q_im[..., None] * b_re[None]
    k = jnp.arange(L + 1, dtype=F32)[:, None, None, None]
    pmag = jnp.exp(k * (lr * step))
    pw_re, pw_im = pmag * jnp.cos(k * (li * step)), pmag * jnp.sin(k * (li * step))
    w_re = pw_re[..., None] * bb_re - pw_im[..., None] * bb_im
    w_im = pw_re[..., None] * bb_im + pw_im[..., None] * bb_re
    kern = (jnp.einsum('gop,ktgpc->ktgoc', c_re, w_re[:L]) - jnp.einsum('gop,ktgpc->ktgoc', c_im, w_im[:L]))
    i = jnp.arange(L)[:, None]
    j = jnp.arange(L)[None, :]
    kf = jnp.where((j >= i)[..., None, None, None], kern[:, 0][jnp.clip(j - i, 0, L - 1)], 0.0)
    kb = jnp.where((i >= j)[..., None, None, None], kern[:, 1][jnp.clip(i - j, 0, L - 1)], 0.0)
    dmat = (jnp.eye(L)[:, :, None, None, None] * jnp.eye(C_GROUP)[None, None, None]
            * d_skip.reshape(G, C_GROUP)[None, None, :, None, :])
    m = (kf + kb + dmat).transpose(2, 0, 4, 1, 3).reshape(G, S5_COLS, S5_COLS)
    to_rows = lambda a: a.transpose(1, 0, 3, 2).reshape(G, S5_COLS, P)
    p_mat = jnp.concatenate([to_rows(w_re[:L, 0][::-1]), to_rows(w_re[:L, 1]),
                             to_rows(w_im[:L, 0][::-1]), to_rows(w_im[:L, 1])], axis=-1)
    def readout(pr, pi):
        wr = c_re[None] * pr[:, :, None, :] - c_im[None] * pi[:, :, None, :]
        wi = c_re[None] * pi[:, :, None, :] + c_im[None] * pr[:, :, None, :]
        fl = lambda a: a.transpose(1, 3, 0, 2).reshape(G, P, S5_COLS)
        return fl(wr), fl(-wi)
    qf_re, qf_im = readout(pw_re[1:, 0], pw_im[1:, 0])
    qb_re, qb_im = readout(pw_re[1:, 1][::-1], pw_im[1:, 1][::-1])
    z = jnp.zeros_like(qf_re)
    qf = jnp.concatenate([qf_re, z, qf_im, z], axis=1)
    qb = jnp.concatenate([z, qb_re, z, qb_im], axis=1)
    w_all = jnp.concatenate([m, qf, qb], axis=1)
    a_re = jnp.concatenate([pw_re[L, 0], pw_re[L, 1]], axis=-1)[:, None, :]
    a_im = jnp.concatenate([pw_im[L, 0], pw_im[L, 1]], axis=-1)[:, None, :]
    return p_mat.astype(BF16), w_all.astype(BF16), a_re, a_im


def _s5_mixer(zc, nb, seq, weights):
    p_mat, w_all, a_re, a_im = weights
    G = p_mat.shape[0]
    nchunks = seq // S5_CHUNK
    rows = -(-nb // SUBLANES) * SUBLANES
    u = zc.reshape(nb, nchunks, S5_CHUNK, G, C_GROUP).transpose(3, 1, 0, 2, 4)
    u = jnp.pad(u, ((0, 0), (0, 0), (0, rows - nb), (0, 0), (0, 0))).reshape(G, nchunks * rows, S5_COLS)
    y = _s5_call(u, p_mat, w_all, a_re, a_im, rows)
    y = y.reshape(G, nchunks, rows, S5_CHUNK, C_GROUP)[:, :, :nb].transpose(2, 1, 3, 0, 4)
    return y.reshape(nb * seq, G * C_GROUP)


def _out_body(*refs, tile_bounds):
    nx = len(tile_bounds)
    x_refs = refs[:nx]
    ya_ref, yb_ref, yc_ref, wg_ref, bg_ref, ogc_ref, wo_ref, o_ref = refs[nx:]
    y = _gelu(yc_ref[...].astype(F32))
    yc = y * _sigmoid(_dot(y.astype(BF16), wg_ref[...]) + bg_ref[...])
    yc = _rms(yc, ogc_ref[...]).astype(BF16)
    acc = _select_rows(x_refs, tile_bounds) + _dot(ya_ref[...], wo_ref[0:A_WIDTH, :])
    acc = acc + _dot(yb_ref[...], wo_ref[A_WIDTH:A_WIDTH + B_WIDTH, :])
    o_ref[...] = acc + _dot(yc, wo_ref[A_WIDTH + B_WIDTH:, :])


def _out_call(xs, ya, yb, yc, w_glu, b_glu, ogc, w_out):
    t, cw = yc.shape
    d = w_out.shape[1]
    tm = TOKEN_TILE
    x_specs, tile_bounds = _row_tile_specs(xs, tm)
    assert tile_bounds[-1] * tm == t
    row = lambda w: pl.BlockSpec((tm, w), lambda i: (i, 0))
    return pl.pallas_call(
        functools.partial(_out_body, tile_bounds=tile_bounds),
        grid=(t // tm,),
        in_specs=x_specs + [row(A_WIDTH), row(B_WIDTH), row(cw), _const_spec((cw, cw)), _const_spec((1, cw)),
                            _const_spec((1, cw)), _const_spec(w_out.shape)],
        out_specs=row(d),
        out_shape=jax.ShapeDtypeStruct((t, d), F32),
        compiler_params=_params("parallel"),
        name="glu_out_proj",
    )(*xs, ya, yb, yc, w_glu, b_glu, ogc, w_out)


def _ffn_body(x_ref, g2_ref, wg_ref, wu_ref, wd_ref, gf_ref, o_ref, h_scr, *, final_norm):
    f = pl.program_id(1)

    @pl.when(f == 0)
    def _():
        x = x_ref[...]
        h_scr[...] = _rms(x, g2_ref[...]).astype(BF16)
        o_ref[...] = x

    h = h_scr[...]
    gate = _dot(h, wg_ref[...])
    a = (gate * _sigmoid(gate) * _dot(h, wu_ref[...])).astype(BF16)
    o_ref[...] += _dot(a, wd_ref[...])

    if final_norm:
        @pl.when(f == pl.num_programs(1) - 1)
        def _():
            o_ref[...] = _rms(o_ref[...], gf_ref[...])


def _ffn_call(x, g2, w_gate, w_up, w_down, gf, final_norm, row_start=0, n_rows=None):
    d = x.shape[1]
    ff = w_gate.shape[1]
    n_rows = x.shape[0] if n_rows is None else n_rows
    tm = TOKEN_TILE
    tf = FF_TILE
    assert n_rows % tm == 0 and row_start % tm == 0 and ff % tf == 0
    first = row_start // tm
    return pl.pallas_call(
        functools.partial(_ffn_body, final_norm=final_norm),
        grid=(n_rows // tm, ff // tf),
        in_specs=[pl.BlockSpec((tm, d), lambda i, f: (i + first, 0)), _const_spec((1, d)),
                  pl.BlockSpec((d, tf), lambda i, f: (0, f)), pl.BlockSpec((d, tf), lambda i, f: (0, f)),
                  pl.BlockSpec((tf, d), lambda i, f: (f, 0)), _const_spec((1, d))],
        out_specs=pl.BlockSpec((tm, d), lambda i, f: (i, 0)),
        out_shape=jax.ShapeDtypeStruct((n_rows, d), F32),
        scratch_shapes=[pltpu.VMEM((tm, d), BF16)],
        compiler_params=_params("parallel", "arbitrary"),
        name="swiglu_ffn",
    )(x, g2, w_gate, w_up, w_down, gf)


def _dft_matrices(n):
    s = jnp.arange(n, dtype=jnp.int32)[None, :]
    unit = 2.0 * math.pi / n
    r = math.isqrt(n)
    if r * r != n:
        ang = ((jnp.arange(n, dtype=jnp.int32)[:, None] * s) % n).astype(F32) * unit
        return jnp.cos(ang), -jnp.sin(ang)
    ab = jnp.arange(r, dtype=jnp.int32)[:, None]
    alpha = (((r * ab) * s) % n).astype(F32) * unit
    beta = ((ab * s) % n).astype(F32) * unit
    ca, sa, cb, sb = jnp.cos(alpha), jnp.sin(alpha), jnp.cos(beta), jnp.sin(beta)
    cos = ca[:, None, :] * cb[None, :, :] - sa[:, None, :] * sb[None, :, :]
    sin = sa[:, None, :] * cb[None, :, :] + ca[:, None, :] * sb[None, :, :]
    return cos.reshape(n, n), -sin.reshape(n, n)


def kernel(x_prompt, x_sample, norm1_g, w_in, a_v_g, a_ws, a_bs, c_lam_re, c_lam_im, c_log_step, c_b_re, c_b_im,
           c_c_re, c_c_im, c_d, c_w_glu, c_b_glu, out_norm_g, w_out, norm2_g, w_gate, w_up, w_down, final_g):
    seq, d = x_prompt.shape[1:]
    assert x_sample.shape[1:] == (seq, d)
    nb = x_prompt.shape[0] + x_sample.shape[0]
    n_prompt = x_prompt.shape[0] * seq
    depth = w_in.shape[0]
    xs = (x_prompt.reshape(-1, d), x_sample.reshape(-1, d))

    cs, ssn = _dft_matrices(seq)
    cs, ssn = cs.astype(BF16), ssn.astype(BF16)
    cd, sdn = _dft_matrices(B_HEAD_DIM)
    cds = jnp.concatenate([cd, -sdn], axis=1).astype(BF16)
    row = lambda v: v.reshape(1, -1)

    for l in range(depth):
        og = out_norm_g[l]
        ws = a_ws[l].astype(BF16)
        bs = jnp.broadcast_to(a_bs[l][:, :, None], (A_HEADS, CHUNK, A_HEAD_DIM))
        ya, xc, xsin, zc = _in_call(xs, row(norm1_g[l]), w_in[l].astype(BF16), row(a_v_g[l]), ws, bs,
                                    row(og[:A_WIDTH]), cds)
        yb = _dft_call(cs, ssn, xc, xsin, row(og[A_WIDTH:A_WIDTH + B_WIDTH]), seq)
        s5w = _s5_weights(c_lam_re[l], c_lam_im[l], c_log_step[l], c_b_re[l], c_b_im[l], c_c_re[l], c_c_im[l],
                          c_d[l])
        yc = _s5_mixer(zc, nb, seq, s5w)
        x = _out_call(xs, ya, yb, yc, c_w_glu[l].astype(BF16), row(c_b_glu[l]), row(og[A_WIDTH + B_WIDTH:]),
                      w_out[l].astype(BF16))
        ffn = functools.partial(_ffn_call, x, row(norm2_g[l]), w_gate[l].astype(BF16), w_up[l].astype(BF16),
                                w_down[l].astype(BF16), row(final_g))
        if l < depth - 1:
            xs = (ffn(final_norm=False),)
        else:
            y_prompt = ffn(final_norm=True, row_start=0, n_rows=n_prompt)
            y_sample = ffn(final_norm=True, row_start=n_prompt, n_rows=x.shape[0] - n_prompt)

    return (y_prompt.reshape(x_prompt.shape), y_sample.reshape(x_sample.shape))
```

```python
import functools
import math

import jax
import jax.numpy as jnp
from jax import lax
from jax.experimental import pallas as pl
from jax.experimental.pallas import tpu as pltpu

F32 = jnp.float32
BF16 = jnp.bfloat16

EPS = 1e-6
A_HEADS = 6
A_HEAD_DIM = 128
A_WIDTH = A_HEADS * A_HEAD_DIM
CHUNK = 128
B_HEADS = 4
B_HEAD_DIM = 128
B_WIDTH = B_HEADS * B_HEAD_DIM
C_GROUP = 16
C_STATE = 64

LANES = 128
SUBLANES = 8
S5_CHUNK = 16
S5_COLS = S5_CHUNK * C_GROUP
VMEM_LIMIT_BYTES = 56 * 1024 * 1024

TOKEN_TILE = 512
FFN_TOKEN_TILE = 1024
FF_TILE = 512
DFT_ROW_TILE = 512
GELU_C = math.sqrt(2.0 / math.pi)


def _gelu(x):
    return 0.5 * x * (1.0 + jnp.tanh(GELU_C * (x + 0.044715 * (x * x * x))))


def _sigmoid(x):
    return 1.0 / (1.0 + jnp.exp(-x))


def _rms(x, g):
    return x * lax.rsqrt(jnp.mean(x * x, axis=-1, keepdims=True) + EPS) * g


def _dot(a, b):
    return jnp.dot(a, b, preferred_element_type=F32)


def _const_spec(shape):
    nd = len(shape)
    return pl.BlockSpec(shape, lambda *_: (0,) * nd, pipeline_mode=pl.Buffered(1))


def _row_tile_specs(xs, tm):
    specs, bounds, start = [], [], 0
    for x in xs:
        n = x.shape[0] // tm
        assert n * tm == x.shape[0] and tm % CHUNK == 0, (x.shape, tm)
        specs.append(pl.BlockSpec((tm, x.shape[1]),
                                  lambda i, *_, start=start, n=n: (jnp.clip(i - start, 0, n - 1), 0)))
        start += n
        bounds.append(start)
    return specs, tuple(bounds)


def _select_rows(x_refs, bounds):
    x = x_refs[-1][...]
    for k in range(len(x_refs) - 2, -1, -1):
        x = jnp.where(pl.program_id(0) < bounds[k], x_refs[k][...], x)
    return x


def _params(*sem):
    return pltpu.CompilerParams(dimension_semantics=sem, vmem_limit_bytes=VMEM_LIMIT_BYTES)


def _in_body(*refs, tile_bounds):
    nx = len(tile_bounds)
    x_refs = refs[:nx]
    (g1_ref, w_ref, vg_ref, ws_ref, bs_ref, oga_ref, cds_ref,
     ya_ref, xc_ref, xs_ref, zc_ref, z_scr, ya_scr) = refs[nx:]
    tm = ya_ref.shape[0]
    nch = tm // CHUNK
    h = _rms(_select_rows(x_refs, tile_bounds), g1_ref[...]).astype(BF16)
    z_scr[...] = _dot(h, w_ref[:, 0:2 * A_WIDTH])
    for hd in range(A_HEADS):
        lo = hd * A_HEAD_DIM
        u = _gelu(z_scr[:, lo:lo + A_HEAD_DIM])
        v = _gelu(z_scr[:, A_WIDTH + lo:A_WIDTH + lo + A_HEAD_DIM])
        v = _rms(v, vg_ref[:, lo:lo + A_HEAD_DIM]).astype(BF16)
        v_wide = jnp.concatenate([v[c * CHUNK:(c + 1) * CHUNK, :] for c in range(nch)], axis=1)
        mixed = _dot(ws_ref[hd], v_wide)
        for c in range(nch):
            r = c * CHUNK
            ya_scr[r:r + CHUNK, lo:lo + A_HEAD_DIM] = u[r:r + CHUNK, :] * (
                mixed[:, c * A_HEAD_DIM:(c + 1) * A_HEAD_DIM] + bs_ref[hd])
    ya_ref[...] = _rms(ya_scr[...], oga_ref[...]).astype(BF16)
    zb = _dot(h, w_ref[:, 2 * A_WIDTH:2 * A_WIDTH + B_WIDTH]).astype(BF16)
    for g in range(B_HEADS):
        lo = g * B_HEAD_DIM
        t = _dot(zb[:, lo:lo + B_HEAD_DIM], cds_ref[...])
        xc_ref[:, lo:lo + B_HEAD_DIM] = t[:, :B_HEAD_DIM].astype(BF16)
        xs_ref[:, lo:lo + B_HEAD_DIM] = t[:, B_HEAD_DIM:].astype(BF16)
    zc_ref[...] = _dot(h, w_ref[:, 2 * A_WIDTH + B_WIDTH:])


def _in_call(xs, g1, w_in, vg, ws, bs, oga, cds):
    d = xs[0].shape[1]
    n_in = w_in.shape[1]
    c_width = n_in - 2 * A_WIDTH - B_WIDTH
    tm = TOKEN_TILE
    x_specs, tile_bounds = _row_tile_specs(xs, tm)
    t = tile_bounds[-1] * tm
    row = lambda w: pl.BlockSpec((tm, w), lambda i: (i, 0))
    return pl.pallas_call(
        functools.partial(_in_body, tile_bounds=tile_bounds),
        grid=(t // tm,),
        in_specs=x_specs + [_const_spec((1, d)), _const_spec((d, n_in)), _const_spec((1, A_WIDTH)),
                            _const_spec(ws.shape), _const_spec(bs.shape), _const_spec((1, A_WIDTH)),
                            _const_spec(cds.shape)],
        out_specs=[row(A_WIDTH), row(B_WIDTH), row(B_WIDTH), row(c_width)],
        out_shape=[jax.ShapeDtypeStruct((t, A_WIDTH), BF16), jax.ShapeDtypeStruct((t, B_WIDTH), BF16),
                   jax.ShapeDtypeStruct((t, B_WIDTH), BF16), jax.ShapeDtypeStruct((t, c_width), F32)],
        scratch_shapes=[pltpu.VMEM((tm, 2 * A_WIDTH), F32), pltpu.VMEM((tm, A_WIDTH), F32)],
        compiler_params=_params("parallel"),
        name="in_proj_gmlp",
    )(*xs, g1, w_in, vg, ws, bs, oga, cds)


def _dft_body(cs_ref, ssn_ref, xc_ref, xs_ref, og_ref, o_ref, *, scale):
    y = _dot(cs_ref[...], xc_ref[...]) + _dot(ssn_ref[...], xs_ref[...])
    o_ref[...] = _rms(y * scale, og_ref[...]).astype(BF16)


def _dft_call(cs, ssn, xc, xs, ogb, seq):
    t, w = xc.shape
    nb = t // seq
    tk = min(DFT_ROW_TILE, seq)
    nk = seq // tk
    scale = 1.0 / math.sqrt(seq * B_HEAD_DIM)
    lhs = pl.BlockSpec((tk, seq), lambda k, b: (k, 0))
    rhs = pl.BlockSpec((seq, w), lambda k, b: (b, 0))
    return pl.pallas_call(
        functools.partial(_dft_body, scale=scale),
        grid=(nk, nb),
        in_specs=[lhs, lhs, rhs, rhs, pl.BlockSpec((1, w), lambda k, b: (0, 0))],
        out_specs=pl.BlockSpec((tk, w), lambda k, b: (b * nk + k, 0)),
        out_shape=jax.ShapeDtypeStruct((t, w), BF16),
        compiler_params=_params("parallel", "parallel"),
        name="fnet_seq_dft",
    )(cs, ssn, xc, xs, ogb)


def _s5_body(u_ref, p_ref, w_ref, are_ref, aim_ref, y_ref, s_scr, ha_scr, hb_scr, *, rows):
    n = u_ref.shape[1]
    nchunks = n // rows
    u = u_ref[0]
    s_scr[...] = _dot(u, p_ref[0])
    ar = jnp.broadcast_to(are_ref[0], (rows, LANES))
    ai = jnp.broadcast_to(aim_ref[0], (rows, LANES))
    is_fwd = lax.broadcasted_iota(jnp.int32, (rows, LANES), 1) < C_STATE
    ha_scr[0:rows, :] = jnp.zeros((rows, 2 * LANES), F32)
    hb_scr[n - rows:n, :] = jnp.zeros((rows, 2 * LANES), F32)

    def step(c, carry):
        hre, him = carry
        r_f = pl.multiple_of(c * rows, rows)
        r_b = pl.multiple_of((nchunks - 1 - c) * rows, rows)
        sa = s_scr[pl.ds(r_f, rows), :]
        sb = s_scr[pl.ds(r_b, rows), :]
        sre = jnp.where(is_fwd, sa[:, :LANES], sb[:, :LANES])
        sim = jnp.where(is_fwd, sa[:, LANES:], sb[:, LANES:])
        nre = hre * ar - him * ai + sre
        nim = hre * ai + him * ar + sim
        hcat = jnp.concatenate([nre, nim], axis=1)
        ha_scr[pl.ds(pl.multiple_of(r_f + rows, rows), rows), :] = hcat
        hb_scr[pl.ds(pl.multiple_of(r_b - rows, rows), rows), :] = hcat
        return nre, nim

    zero = jnp.zeros((rows, LANES), F32)
    lax.fori_loop(0, nchunks - 1, step, (zero, zero))
    y = (_dot(u, w_ref[0, 0:S5_COLS, :])
         + _dot(ha_scr[...].astype(BF16), w_ref[0, S5_COLS:2 * S5_COLS, :])
         + _dot(hb_scr[...].astype(BF16), w_ref[0, 2 * S5_COLS:3 * S5_COLS, :]))
    y_ref[0] = y.astype(BF16)


def _s5_call(uc, p, w, are, aim, rows):
    g, n, cols = uc.shape
    blk = lambda a: pl.BlockSpec((1,) + a.shape[1:], lambda i: (i, 0, 0))
    return pl.pallas_call(
        functools.partial(_s5_body, rows=rows),
        grid=(g,),
        in_specs=[blk(uc), blk(p), blk(w), blk(are), blk(aim)],
        out_specs=pl.BlockSpec((1, n, cols), lambda i: (i, 0, 0)),
        out_shape=jax.ShapeDtypeStruct((g, n, cols), BF16),
        scratch_shapes=[pltpu.VMEM((n, 2 * LANES), F32)] * 3,
        compiler_params=_params("parallel"),
        name="s5_chunk_scan",
    )(uc, p, w, are, aim)


def _s5_weights(lam_re, lam_im, log_step, b_re, b_im, c_re, c_im, d_skip):
    L = S5_CHUNK
    G, P = lam_re.shape[1:]
    step = jnp.exp(log_step)[..., None]
    lr, li = lam_re, lam_im
    mag = jnp.exp(lr * step)
    ab_re, ab_im = mag * jnp.cos(li * step), mag * jnp.sin(li * step)
    den = lr * lr + li * li
    nr = ab_re - 1.0
    q_re = (nr * lr + ab_im * li) / den
    q_im = (ab_im * lr - nr * li) / den
    bb_re = q_re[..., None] * b_re[None] - q_im[..., None] * b_im[None]
    bb_im = q_re[..., None] * b_im[None] + q_im[..., None] * b_re[None]
    k = jnp.arange(L + 1, dtype=F32)[:, None, None, None]
    pmag = jnp.exp(k * (lr * step))
    pw_re, pw_im = pmag * jnp.cos(k * (li * step)), pmag * jnp.sin(k * (li * step))
    w_re = pw_re[..., None] * bb_re - pw_im[..., None] * bb_im
    w_im = pw_re[..., None] * bb_im + pw_im[..., None] * bb_re
    kern = (jnp.einsum('gop,ktgpc->ktgoc', c_re, w_re[:L]) - jnp.einsum('gop,ktgpc->ktgoc', c_im, w_im[:L]))
    i = jnp.arange(L)[:, None]
    j = jnp.arange(L)[None, :]
    kf = jnp.where((j >= i)[..., None, None, None], kern[:, 0][jnp.clip(j - i, 0, L - 1)], 0.0)
    kb = jnp.where((i >= j)[..., None, None, None], kern[:, 1][jnp.clip(i - j, 0, L - 1)], 0.0)
    dmat = (jnp.eye(L)[:, :, None, None, None] * jnp.eye(C_GROUP)[None, None, None]
            * d_skip.reshape(G, C_GROUP)[None, None, :, None, :])
    m = (kf + kb + dmat).transpose(2, 0, 4, 1, 3).reshape(G, S5_COLS, S5_COLS)
    to_rows = lambda a: a.transpose(1, 0, 3, 2).reshape(G, S5_COLS, P)
    p_mat = jnp.concatenate([to_rows(w_re[:L, 0][::-1]), to_rows(w_re[:L, 1]),
                             to_rows(w_im[:L, 0][::-1]), to_rows(w_im[:L, 1])], axis=-1)
    def readout(pr, pi):
        wr = c_re[None] * pr[:, :, None, :] - c_im[None] * pi[:, :, None, :]
        wi = c_re[None] * pi[:, :, None, :] + c_im[None] * pr[:, :, None, :]
        fl = lambda a: a.transpose(1, 3, 0, 2).reshape(G, P, S5_COLS)
        return fl(wr), fl(-wi)
    qf_re, qf_im = readout(pw_re[1:, 0], pw_im[1:, 0])
    qb_re, qb_im = readout(pw_re[1:, 1][::-1], pw_im[1:, 1][::-1])
    z = jnp.zeros_like(qf_re)
    qf = jnp.concatenate([qf_re, z, qf_im, z], axis=1)
    qb = jnp.concatenate([z, qb_re, z, qb_im], axis=1)
    w_all = jnp.concatenate([m, qf, qb], axis=1)
    a_re = jnp.concatenate([pw_re[L, 0], pw_re[L, 1]], axis=-1)[:, None, :]
    a_im = jnp.concatenate([pw_im[L, 0], pw_im[L, 1]], axis=-1)[:, None, :]
    return p_mat.astype(BF16), w_all.astype(BF16), a_re, a_im


GROUPS_PER_BLOCK = LANES // C_GROUP


def _s5_pack_body(z_ref, u_ref, v_scr, *, nb):
    nchunks = u_ref.shape[1]

    @pl.when(pl.program_id(1) < nb)
    def _():
        for i in range(S5_CHUNK):
            t = z_ref[0, pl.ds(i, nchunks, stride=S5_CHUNK), :].T
            for g in range(GROUPS_PER_BLOCK):
                v_scr[g, i * C_GROUP:(i + 1) * C_GROUP, :] = t[g * C_GROUP:(g + 1) * C_GROUP, :]
        for g in range(GROUPS_PER_BLOCK):
            u_ref[g] = v_scr[g].T.astype(BF16)

    @pl.when(pl.program_id(1) >= nb)
    def _():
        u_ref[...] = jnp.zeros(u_ref.shape, BF16)


def _s5_pack_call(zc, nb, seq, rows):
    width = zc.shape[1]
    nchunks = seq // S5_CHUNK
    ngroups = width // C_GROUP
    return pl.pallas_call(
        functools.partial(_s5_pack_body, nb=nb),
        grid=(width // LANES, rows),
        in_specs=[pl.BlockSpec((1, seq, LANES), lambda b, n: (jnp.minimum(n, nb - 1), 0, b))],
        out_specs=pl.BlockSpec((GROUPS_PER_BLOCK, nchunks, S5_COLS), lambda b, n: (b, 0, n)),
        out_shape=jax.ShapeDtypeStruct((ngroups, nchunks, rows * S5_COLS), BF16),
        scratch_shapes=[pltpu.VMEM((GROUPS_PER_BLOCK, S5_COLS, nchunks), F32)],
        compiler_params=_params("parallel", "parallel"),
        name="s5_pack",
    )(zc.reshape(nb, seq, width))


def _s5_unpack_body(y_ref, o_ref, w_scr):
    nchunks = y_ref.shape[1]
    for g in range(GROUPS_PER_BLOCK):
        t = y_ref[g].astype(F32).T
        for j in range(S5_CHUNK):
            w_scr[j, g * C_GROUP:(g + 1) * C_GROUP, :] = t[j * C_GROUP:(j + 1) * C_GROUP, :]
    for j in range(S5_CHUNK):
        o_ref[0, pl.ds(j, nchunks, stride=S5_CHUNK), :] = w_scr[j].T


def _s5_unpack_call(y, nb, seq):
    ngroups, nchunks, _ = y.shape
    width = ngroups * C_GROUP
    return pl.pallas_call(
        _s5_unpack_body,
        grid=(width // LANES, nb),
        in_specs=[pl.BlockSpec((GROUPS_PER_BLOCK, nchunks, S5_COLS), lambda b, n: (b, 0, n))],
        out_specs=pl.BlockSpec((1, seq, LANES), lambda b, n: (n, 0, b)),
        out_shape=jax.ShapeDtypeStruct((nb, seq, width), F32),
        scratch_shapes=[pltpu.VMEM((S5_CHUNK, LANES, nchunks), F32)],
        compiler_params=_params("parallel", "parallel"),
        name="s5_unpack",
    )(y).reshape(nb * seq, width)


def _s5_mixer(zc, nb, seq, weights):
    p_mat, w_all, a_re, a_im = weights
    G = p_mat.shape[0]
    nchunks = seq // S5_CHUNK
    rows = -(-nb // SUBLANES) * SUBLANES
    u = _s5_pack_call(zc, nb, seq, rows).reshape(G, nchunks * rows, S5_COLS)
    y = _s5_call(u, p_mat, w_all, a_re, a_im, rows)
    return _s5_unpack_call(y.reshape(G, nchunks, rows * S5_COLS), nb, seq)


def _out_body(*refs, tile_bounds):
    nx = len(tile_bounds)
    x_refs = refs[:nx]
    ya_ref, yb_ref, yc_ref, wg_ref, bg_ref, ogc_ref, wo_ref, o_ref = refs[nx:]
    y = _gelu(yc_ref[...].astype(F32))
    yc = y * _sigmoid(_dot(y.astype(BF16), wg_ref[...]) + bg_ref[...])
    yc = _rms(yc, ogc_ref[...]).astype(BF16)
    acc = _select_rows(x_refs, tile_bounds) + _dot(ya_ref[...], wo_ref[0:A_WIDTH, :])
    acc = acc + _dot(yb_ref[...], wo_ref[A_WIDTH:A_WIDTH + B_WIDTH, :])
    o_ref[...] = acc + _dot(yc, wo_ref[A_WIDTH + B_WIDTH:, :])


def _out_call(xs, ya, yb, yc, w_glu, b_glu, ogc, w_out):
    t, cw = yc.shape
    d = w_out.shape[1]
    tm = TOKEN_TILE
    x_specs, tile_bounds = _row_tile_specs(xs, tm)
    assert tile_bounds[-1] * tm == t
    row = lambda w: pl.BlockSpec((tm, w), lambda i: (i, 0))
    return pl.pallas_call(
        functools.partial(_out_body, tile_bounds=tile_bounds),
        grid=(t // tm,),
        in_specs=x_specs + [row(A_WIDTH), row(B_WIDTH), row(cw), _const_spec((cw, cw)), _const_spec((1, cw)),
                            _const_spec((1, cw)), _const_spec(w_out.shape)],
        out_specs=row(d),
        out_shape=jax.ShapeDtypeStruct((t, d), F32),
        compiler_params=_params("parallel"),
        name="glu_out_proj",
    )(*xs, ya, yb, yc, w_glu, b_glu, ogc, w_out)


def _ffn_body(x_ref, g2_ref, wg_ref, wu_ref, wd_ref, gf_ref, o_ref, h_scr, *, final_norm):
    f = pl.program_id(1)

    @pl.when(f == 0)
    def _():
        x = x_ref[...]
        h_scr[...] = _rms(x, g2_ref[...]).astype(BF16)
        o_ref[...] = x

    h = h_scr[...]
    gate = _dot(h, wg_ref[...])
    a = (gate * _sigmoid(gate) * _dot(h, wu_ref[...])).astype(BF16)
    o_ref[...] += _dot(a, wd_ref[...])

    if final_norm:
        @pl.when(f == pl.num_programs(1) - 1)
        def _():
            o_ref[...] = _rms(o_ref[...], gf_ref[...])


def _ffn_call(x, g2, w_gate, w_up, w_down, gf, final_norm, row_start=0, n_rows=None):
    d = x.shape[1]
    ff = w_gate.shape[1]
    n_rows = x.shape[0] if n_rows is None else n_rows
    tm = min(FFN_TOKEN_TILE, n_rows)
    tf = FF_TILE
    assert n_rows % tm == 0 and row_start % tm == 0 and ff % tf == 0
    first = row_start // tm
    return pl.pallas_call(
        functools.partial(_ffn_body, final_norm=final_norm),
        grid=(n_rows // tm, ff // tf),
        in_specs=[pl.BlockSpec((tm, d), lambda i, f: (i + first, 0)), _const_spec((1, d)),
                  pl.BlockSpec((d, tf), lambda i, f: (0, f)), pl.BlockSpec((d, tf), lambda i, f: (0, f)),
                  pl.BlockSpec((tf, d), lambda i, f: (f, 0)), _const_spec((1, d))],
        out_specs=pl.BlockSpec((tm, d), lambda i, f: (i, 0)),
        out_shape=jax.ShapeDtypeStruct((n_rows, d), F32),
        scratch_shapes=[pltpu.VMEM((tm, d), BF16)],
        compiler_params=_params("parallel", "arbitrary"),
        name="swiglu_ffn",
    )(x, g2, w_gate, w_up, w_down, gf)


def _dft_matrices(n):
    s = jnp.arange(n, dtype=jnp.int32)[None, :]
    unit = 2.0 * math.pi / n
    r = math.isqrt(n)
    if r * r != n:
        ang = ((jnp.arange(n, dtype=jnp.int32)[:, None] * s) % n).astype(F32) * unit
        return jnp.cos(ang), -jnp.sin(ang)
    ab = jnp.arange(r, dtype=jnp.int32)[:, None]
    alpha = (((r * ab) * s) % n).astype(F32) * unit
    beta = ((ab * s) % n).astype(F32) * unit
    ca, sa, cb, sb = jnp.cos(alpha), jnp.sin(alpha), jnp.cos(beta), jnp.sin(beta)
    cos = ca[:, None, :] * cb[None, :, :] - sa[:, None, :] * sb[None, :, :]
    sin = sa[:, None, :] * cb[None, :, :] + ca[:, None, :] * sb[None, :, :]
    return cos.reshape(n, n), -sin.reshape(n, n)


def kernel(x_prompt, x_sample, norm1_g, w_in, a_v_g, a_ws, a_bs, c_lam_re, c_lam_im, c_log_step, c_b_re, c_b_im,
           c_c_re, c_c_im, c_d, c_w_glu, c_b_glu, out_norm_g, w_out, norm2_g, w_gate, w_up, w_down, final_g):
    seq, d = x_prompt.shape[1:]
    assert x_sample.shape[1:] == (seq, d)
    nb = x_prompt.shape[0] + x_sample.shape[0]
    n_prompt = x_prompt.shape[0] * seq
    depth = w_in.shape[0]
    xs = (x_prompt.reshape(-1, d), x_sample.reshape(-1, d))

    cs, ssn = _dft_matrices(seq)
    cs, ssn = cs.astype(BF16), ssn.astype(BF16)
    cd, sdn = _dft_matrices(B_HEAD_DIM)
    cds = jnp.concatenate([cd, -sdn], axis=1).astype(BF16)
    row = lambda v: v.reshape(1, -1)

    for l in range(depth):
        og = out_norm_g[l]
        ws = a_ws[l].astype(BF16)
        bs = jnp.broadcast_to(a_bs[l][:, :, None], (A_HEADS, CHUNK, A_HEAD_DIM))
        ya, xc, xsin, zc = _in_call(xs, row(norm1_g[l]), w_in[l].astype(BF16), row(a_v_g[l]), ws, bs,
                                    row(og[:A_WIDTH]), cds)
        yb = _dft_call(cs, ssn, xc, xsin, row(og[A_WIDTH:A_WIDTH + B_WIDTH]), seq)
        s5w = _s5_weights(c_lam_re[l], c_lam_im[l], c_log_step[l], c_b_re[l], c_b_im[l], c_c_re[l], c_c_im[l],
                          c_d[l])
        yc = _s5_mixer(zc, nb, seq, s5w)
        x = _out_call(xs, ya, yb, yc, c_w_glu[l].astype(BF16), row(c_b_glu[l]), row(og[A_WIDTH + B_WIDTH:]),
                      w_out[l].astype(BF16))
        ffn = functools.partial(_ffn_call, x, row(norm2_g[l]), w_gate[l].astype(BF16), w_up[l].astype(BF16),
                                w_down[l].astype(BF16), row(final_g))
        if l < depth - 1:
            xs = (ffn(final_norm=False),)
        else:
            y_prompt = ffn(final_norm=True, row_start=0, n_rows=n_prompt)
            y_sample = ffn(final_norm=True, row_start=n_prompt, n_rows=x.shape[0] - n_prompt)

    return (y_prompt.reshape(x_prompt.shape), y_sample.reshape(x_sample.shape))
```

```python
import functools
import math

import jax
import jax.numpy as jnp
from jax import lax
from jax.experimental import pallas as pl
from jax.experimental.pallas import tpu as pltpu

F32 = jnp.float32
BF16 = jnp.bfloat16

EPS = 1e-6
A_HEADS = 6
A_HEAD_DIM = 128
A_WIDTH = A_HEADS * A_HEAD_DIM
CHUNK = 128
B_HEADS = 4
B_HEAD_DIM = 128
B_WIDTH = B_HEADS * B_HEAD_DIM
C_GROUP = 16
C_STATE = 64

LANES = 128
SUBLANES = 8
S5_CHUNK = 16
S5_COLS = S5_CHUNK * C_GROUP
VMEM_LIMIT_BYTES = 56 * 1024 * 1024

TOKEN_TILE = 512
FFN_TOKEN_TILE = 1024
FF_TILE = 512
DFT_ROW_TILE = 512
GELU_C = math.sqrt(2.0 / math.pi)


def _gelu(x):
    return 0.5 * x * (1.0 + jnp.tanh(GELU_C * (x + 0.044715 * (x * x * x))))


def _sigmoid(x):
    return 1.0 / (1.0 + jnp.exp(-x))


def _rms(x, g):
    return x * lax.rsqrt(jnp.mean(x * x, axis=-1, keepdims=True) + EPS) * g


def _dot(a, b):
    return jnp.dot(a, b, preferred_element_type=F32)


def _const_spec(shape):
    nd = len(shape)
    return pl.BlockSpec(shape, lambda *_: (0,) * nd, pipeline_mode=pl.Buffered(1))


def _row_tile_specs(xs, tm):
    specs, bounds, start = [], [], 0
    for x in xs:
        n = x.shape[0] // tm
        assert n * tm == x.shape[0] and tm % CHUNK == 0, (x.shape, tm)
        specs.append(pl.BlockSpec((tm, x.shape[1]),
                                  lambda i, *_, start=start, n=n: (jnp.clip(i - start, 0, n - 1), 0)))
        start += n
        bounds.append(start)
    return specs, tuple(bounds)


def _select_rows(x_refs, bounds):
    x = x_refs[-1][...]
    for k in range(len(x_refs) - 2, -1, -1):
        x = jnp.where(pl.program_id(0) < bounds[k], x_refs[k][...], x)
    return x


def _params(*sem):
    return pltpu.CompilerParams(dimension_semantics=sem, vmem_limit_bytes=VMEM_LIMIT_BYTES)


def _in_body(*refs, tile_bounds):
    nx = len(tile_bounds)
    x_refs = refs[:nx]
    (g1_ref, w_ref, vg_ref, ws_ref, bs_ref, oga_ref, cds_ref,
     ya_ref, xc_ref, xs_ref, zc_ref, z_scr, ya_scr) = refs[nx:]
    tm = ya_ref.shape[0]
    nch = tm // CHUNK
    h = _rms(_select_rows(x_refs, tile_bounds), g1_ref[...]).astype(BF16)
    z_scr[...] = _dot(h, w_ref[:, 0:2 * A_WIDTH])
    for hd in range(A_HEADS):
        lo = hd * A_HEAD_DIM
        u = _gelu(z_scr[:, lo:lo + A_HEAD_DIM])
        v = _gelu(z_scr[:, A_WIDTH + lo:A_WIDTH + lo + A_HEAD_DIM])
        v = _rms(v, vg_ref[:, lo:lo + A_HEAD_DIM]).astype(BF16)
        v_wide = jnp.concatenate([v[c * CHUNK:(c + 1) * CHUNK, :] for c in range(nch)], axis=1)
        mixed = _dot(ws_ref[hd], v_wide)
        for c in range(nch):
            r = c * CHUNK
            ya_scr[r:r + CHUNK, lo:lo + A_HEAD_DIM] = u[r:r + CHUNK, :] * (
                mixed[:, c * A_HEAD_DIM:(c + 1) * A_HEAD_DIM] + bs_ref[hd])
    ya_ref[...] = _rms(ya_scr[...], oga_ref[...]).astype(BF16)
    zb = _dot(h, w_ref[:, 2 * A_WIDTH:2 * A_WIDTH + B_WIDTH]).astype(BF16)
    for g in range(B_HEADS):
        lo = g * B_HEAD_DIM
        t = _dot(zb[:, lo:lo + B_HEAD_DIM], cds_ref[...])
        xc_ref[:, lo:lo + B_HEAD_DIM] = t[:, :B_HEAD_DIM].astype(BF16)
        xs_ref[:, lo:lo + B_HEAD_DIM] = t[:, B_HEAD_DIM:].astype(BF16)
    zc_ref[...] = _dot(h, w_ref[:, 2 * A_WIDTH + B_WIDTH:])


def _in_call(xs, g1, w_in, vg, ws, bs, oga, cds):
    d = xs[0].shape[1]
    n_in = w_in.shape[1]
    c_width = n_in - 2 * A_WIDTH - B_WIDTH
    tm = TOKEN_TILE
    x_specs, tile_bounds = _row_tile_specs(xs, tm)
    t = tile_bounds[-1] * tm
    row = lambda w: pl.BlockSpec((tm, w), lambda i: (i, 0))
    return pl.pallas_call(
        functools.partial(_in_body, tile_bounds=tile_bounds),
        grid=(t // tm,),
        in_specs=x_specs + [_const_spec((1, d)), _const_spec((d, n_in)), _const_spec((1, A_WIDTH)),
                            _const_spec(ws.shape), _const_spec(bs.shape), _const_spec((1, A_WIDTH)),
                            _const_spec(cds.shape)],
        out_specs=[row(A_WIDTH), row(B_WIDTH), row(B_WIDTH), row(c_width)],
        out_shape=[jax.ShapeDtypeStruct((t, A_WIDTH), BF16), jax.ShapeDtypeStruct((t, B_WIDTH), BF16),
                   jax.ShapeDtypeStruct((t, B_WIDTH), BF16), jax.ShapeDtypeStruct((t, c_width), F32)],
        scratch_shapes=[pltpu.VMEM((tm, 2 * A_WIDTH), F32), pltpu.VMEM((tm, A_WIDTH), F32)],
        compiler_params=_params("parallel"),
        name="in_proj_gmlp",
    )(*xs, g1, w_in, vg, ws, bs, oga, cds)


def _dft_body(cs_ref, ssn_ref, xc_ref, xs_ref, og_ref, o_ref, *, scale):
    y = _dot(cs_ref[...], xc_ref[...]) + _dot(ssn_ref[...], xs_ref[...])
    o_ref[...] = _rms(y * scale, og_ref[...]).astype(BF16)


def _dft_call(cs, ssn, xc, xs, ogb, seq):
    t, w = xc.shape
    nb = t // seq
    tk = min(DFT_ROW_TILE, seq)
    nk = seq // tk
    scale = 1.0 / math.sqrt(seq * B_HEAD_DIM)
    lhs = pl.BlockSpec((tk, seq), lambda k, b: (k, 0))
    rhs = pl.BlockSpec((seq, w), lambda k, b: (b, 0))
    return pl.pallas_call(
        functools.partial(_dft_body, scale=scale),
        grid=(nk, nb),
        in_specs=[lhs, lhs, rhs, rhs, pl.BlockSpec((1, w), lambda k, b: (0, 0))],
        out_specs=pl.BlockSpec((tk, w), lambda k, b: (b * nk + k, 0)),
        out_shape=jax.ShapeDtypeStruct((t, w), BF16),
        compiler_params=_params("parallel", "parallel"),
        name="fnet_seq_dft",
    )(cs, ssn, xc, xs, ogb)


def _s5_weights(lam_re, lam_im, log_step, b_re, b_im, c_re, c_im, d_skip):
    L = S5_CHUNK
    G, P = lam_re.shape[1:]
    step = jnp.exp(log_step)[..., None]
    lr, li = lam_re, lam_im
    mag = jnp.exp(lr * step)
    ab_re, ab_im = mag * jnp.cos(li * step), mag * jnp.sin(li * step)
    den = lr * lr + li * li
    nr = ab_re - 1.0
    q_re = (nr * lr + ab_im * li) / den
    q_im = (ab_im * lr - nr * li) / den
    bb_re = q_re[..., None] * b_re[None] - q_im[..., None] * b_im[None]
    bb_im = q_re[..., None] * b_im[None] + q_im[..., None] * b_re[None]
    k = jnp.arange(L + 1, dtype=F32)[:, None, None, None]
    pmag = jnp.exp(k * (lr * step))
    pw_re, pw_im = pmag * jnp.cos(k * (li * step)), pmag * jnp.sin(k * (li * step))
    w_re = pw_re[..., None] * bb_re - pw_im[..., None] * bb_im
    w_im = pw_re[..., None] * bb_im + pw_im[..., None] * bb_re
    kern = (jnp.einsum('gop,ktgpc->ktgoc', c_re, w_re[:L]) - jnp.einsum('gop,ktgpc->ktgoc', c_im, w_im[:L]))
    i = jnp.arange(L)[:, None]
    j = jnp.arange(L)[None, :]
    kf = jnp.where((j >= i)[..., None, None, None], kern[:, 0][jnp.clip(j - i, 0, L - 1)], 0.0)
    kb = jnp.where((i >= j)[..., None, None, None], kern[:, 1][jnp.clip(i - j, 0, L - 1)], 0.0)
    dmat = (jnp.eye(L)[:, :, None, None, None] * jnp.eye(C_GROUP)[None, None, None]
            * d_skip.reshape(G, C_GROUP)[None, None, :, None, :])
    m = (kf + kb + dmat).transpose(2, 0, 4, 1, 3).reshape(G, S5_COLS, S5_COLS)
    to_rows = lambda a: a.transpose(1, 0, 3, 2).reshape(G, S5_COLS, P)
    p_mat = jnp.concatenate([to_rows(w_re[:L, 0][::-1]), to_rows(w_re[:L, 1]),
                             to_rows(w_im[:L, 0][::-1]), to_rows(w_im[:L, 1])], axis=-1)
    def readout(pr, pi):
        wr = c_re[None] * pr[:, :, None, :] - c_im[None] * pi[:, :, None, :]
        wi = c_re[None] * pi[:, :, None, :] + c_im[None] * pr[:, :, None, :]
        fl = lambda a: a.transpose(1, 3, 0, 2).reshape(G, P, S5_COLS)
        return fl(wr), fl(-wi)
    qf_re, qf_im = readout(pw_re[1:, 0], pw_im[1:, 0])
    qb_re, qb_im = readout(pw_re[1:, 1][::-1], pw_im[1:, 1][::-1])
    z = jnp.zeros_like(qf_re)
    qf = jnp.concatenate([qf_re, z, qf_im, z], axis=1)
    qb = jnp.concatenate([z, qb_re, z, qb_im], axis=1)
    w_all = jnp.concatenate([m, qf, qb], axis=1)
    a_re = jnp.concatenate([pw_re[L, 0], pw_re[L, 1]], axis=-1)
    a_im = jnp.concatenate([pw_im[L, 0], pw_im[L, 1]], axis=-1)
    return p_mat.astype(BF16), w_all.astype(BF16), a_re, a_im


GROUPS_PER_BLOCK = LANES // C_GROUP


def _s5_body(z_ref, p_ref, w_ref, are_ref, aim_ref, o_ref, v_scr, u_scr, s_scr, ha_scr, hb_scr, y_scr):
    nchunks = z_ref.shape[1] // S5_CHUNK
    rows = GROUPS_PER_BLOCK
    n = nchunks * rows
    for i in range(S5_CHUNK):
        t = z_ref[0, pl.ds(i, nchunks, stride=S5_CHUNK), :].T
        for g in range(rows):
            v_scr[g, i * C_GROUP:(i + 1) * C_GROUP, :] = t[g * C_GROUP:(g + 1) * C_GROUP, :]
    for g in range(rows):
        u = v_scr[g].T.astype(BF16)
        u_scr[g] = u
        s = _dot(u, p_ref[g])
        for part in range(2):
            s_scr[part, pl.ds(g, nchunks, stride=rows), :] = s[:, part * LANES:(part + 1) * LANES]

    ar = are_ref[...]
    ai = aim_ref[...]
    is_fwd = lax.broadcasted_iota(jnp.int32, (rows, LANES), 1) < C_STATE
    zero = jnp.zeros((rows, LANES), F32)
    for part in range(2):
        ha_scr[part, 0:rows, :] = zero
        hb_scr[part, n - rows:n, :] = zero

    def step(c, carry):
        hre, him = carry
        r_f = pl.multiple_of(c * rows, rows)
        r_b = pl.multiple_of((nchunks - 1 - c) * rows, rows)
        sre = jnp.where(is_fwd, s_scr[0, pl.ds(r_f, rows), :], s_scr[0, pl.ds(r_b, rows), :])
        sim = jnp.where(is_fwd, s_scr[1, pl.ds(r_f, rows), :], s_scr[1, pl.ds(r_b, rows), :])
        nre = hre * ar - him * ai + sre
        nim = hre * ai + him * ar + sim
        for part, h in enumerate((nre, nim)):
            ha_scr[part, pl.ds(pl.multiple_of(r_f + rows, rows), rows), :] = h
            hb_scr[part, pl.ds(pl.multiple_of(r_b - rows, rows), rows), :] = h
        return nre, nim

    lax.fori_loop(0, nchunks - 1, step, (zero, zero))

    def group_rows(scr, g):
        return jnp.concatenate([scr[part, pl.ds(g, nchunks, stride=rows), :] for part in range(2)],
                               axis=1).astype(BF16)

    for g in range(rows):
        ha = group_rows(ha_scr, g)
        hb = group_rows(hb_scr, g)
        y = (_dot(u_scr[g], w_ref[g, 0:S5_COLS, :]) + _dot(ha, w_ref[g, S5_COLS:2 * S5_COLS, :])
             + _dot(hb, w_ref[g, 2 * S5_COLS:3 * S5_COLS, :]))
        t = y.T
        for j in range(S5_CHUNK):
            y_scr[j, g * C_GROUP:(g + 1) * C_GROUP, :] = t[j * C_GROUP:(j + 1) * C_GROUP, :]
    for j in range(S5_CHUNK):
        o_ref[0, pl.ds(j, nchunks, stride=S5_CHUNK), :] = y_scr[j].T


def _s5_mixer(zc, nb, seq, weights):
    p_mat, w_all, a_re, a_im = weights
    width = zc.shape[1]
    nchunks = seq // S5_CHUNK
    gb = GROUPS_PER_BLOCK
    per_block = lambda a: pl.BlockSpec((gb,) + a.shape[1:], lambda b, n: (b,) + (0,) * (a.ndim - 1))
    seq_block = pl.BlockSpec((1, seq, LANES), lambda b, n: (n, 0, b))
    return pl.pallas_call(
        _s5_body,
        grid=(width // LANES, nb),
        in_specs=[seq_block, per_block(p_mat), per_block(w_all), per_block(a_re), per_block(a_im)],
        out_specs=seq_block,
        out_shape=jax.ShapeDtypeStruct((nb, seq, width), F32),
        scratch_shapes=[pltpu.VMEM((gb, S5_COLS, nchunks), F32), pltpu.VMEM((gb, nchunks, S5_COLS), BF16),
                        pltpu.VMEM((2, nchunks * gb, LANES), F32), pltpu.VMEM((2, nchunks * gb, LANES), F32),
                        pltpu.VMEM((2, nchunks * gb, LANES), F32), pltpu.VMEM((S5_CHUNK, LANES, nchunks), F32)],
        compiler_params=_params("parallel", "parallel"),
        name="s5_mixer",
    )(zc.reshape(nb, seq, width), p_mat, w_all, a_re, a_im).reshape(nb * seq, width)


def _out_body(*refs, tile_bounds):
    nx = len(tile_bounds)
    x_refs = refs[:nx]
    ya_ref, yb_ref, yc_ref, wg_ref, bg_ref, ogc_ref, wo_ref, o_ref = refs[nx:]
    y = _gelu(yc_ref[...].astype(F32))
    yc = y * _sigmoid(_dot(y.astype(BF16), wg_ref[...]) + bg_ref[...])
    yc = _rms(yc, ogc_ref[...]).astype(BF16)
    acc = _select_rows(x_refs, tile_bounds) + _dot(ya_ref[...], wo_ref[0:A_WIDTH, :])
    acc = acc + _dot(yb_ref[...], wo_ref[A_WIDTH:A_WIDTH + B_WIDTH, :])
    o_ref[...] = acc + _dot(yc, wo_ref[A_WIDTH + B_WIDTH:, :])


def _out_call(xs, ya, yb, yc, w_glu, b_glu, ogc, w_out):
    t, cw = yc.shape
    d = w_out.shape[1]
    tm = TOKEN_TILE
    x_specs, tile_bounds = _row_tile_specs(xs, tm)
    assert tile_bounds[-1] * tm == t
    row = lambda w: pl.BlockSpec((tm, w), lambda i: (i, 0))
    return pl.pallas_call(
        functools.partial(_out_body, tile_bounds=tile_bounds),
        grid=(t // tm,),
        in_specs=x_specs + [row(A_WIDTH), row(B_WIDTH), row(cw), _const_spec((cw, cw)), _const_spec((1, cw)),
                            _const_spec((1, cw)), _const_spec(w_out.shape)],
        out_specs=row(d),
        out_shape=jax.ShapeDtypeStruct((t, d), F32),
        compiler_params=_params("parallel"),
        name="glu_out_proj",
    )(*xs, ya, yb, yc, w_glu, b_glu, ogc, w_out)


def _ffn_body(x_ref, g2_ref, wg_ref, wu_ref, wd_ref, gf_ref, o_ref, h_scr, *, final_norm):
    f = pl.program_id(1)

    @pl.when(f == 0)
    def _():
        x = x_ref[...]
        h_scr[...] = _rms(x, g2_ref[...]).astype(BF16)
        o_ref[...] = x

    h = h_scr[...]
    gate = _dot(h, wg_ref[...])
    a = (gate * _sigmoid(gate) * _dot(h, wu_ref[...])).astype(BF16)
    o_ref[...] += _dot(a, wd_ref[...])

    if final_norm:
        @pl.when(f == pl.num_programs(1) - 1)
        def _():
            o_ref[...] = _rms(o_ref[...], gf_ref[...])


def _ffn_call(x, g2, w_gate, w_up, w_down, gf, final_norm, row_start=0, n_rows=None):
    d = x.shape[1]
    ff = w_gate.shape[1]
    n_rows = x.shape[0] if n_rows is None else n_rows
    tm = min(FFN_TOKEN_TILE, n_rows)
    tf = FF_TILE
    assert n_rows % tm == 0 and row_start % tm == 0 and ff % tf == 0
    first = row_start // tm
    return pl.pallas_call(
        functools.partial(_ffn_body, final_norm=final_norm),
        grid=(n_rows // tm, ff // tf),
        in_specs=[pl.BlockSpec((tm, d), lambda i, f: (i + first, 0)), _const_spec((1, d)),
                  pl.BlockSpec((d, tf), lambda i, f: (0, f)), pl.BlockSpec((d, tf), lambda i, f: (0, f)),
                  pl.BlockSpec((tf, d), lambda i, f: (f, 0)), _const_spec((1, d))],
        out_specs=pl.BlockSpec((tm, d), lambda i, f: (i, 0)),
        out_shape=jax.ShapeDtypeStruct((n_rows, d), F32),
        scratch_shapes=[pltpu.VMEM((tm, d), BF16)],
        compiler_params=_params("parallel", "arbitrary"),
        name="swiglu_ffn",
    )(x, g2, w_gate, w_up, w_down, gf)


def _dft_matrices(n):
    s = jnp.arange(n, dtype=jnp.int32)[None, :]
    unit = 2.0 * math.pi / n
    r = math.isqrt(n)
    if r * r != n:
        ang = ((jnp.arange(n, dtype=jnp.int32)[:, None] * s) % n).astype(F32) * unit
        return jnp.cos(ang), -jnp.sin(ang)
    ab = jnp.arange(r, dtype=jnp.int32)[:, None]
    alpha = (((r * ab) * s) % n).astype(F32) * unit
    beta = ((ab * s) % n).astype(F32) * unit
    ca, sa, cb, sb = jnp.cos(alpha), jnp.sin(alpha), jnp.cos(beta), jnp.sin(beta)
    cos = ca[:, None, :] * cb[None, :, :] - sa[:, None, :] * sb[None, :, :]
    sin = sa[:, None, :] * cb[None, :, :] + ca[:, None, :] * sb[None, :, :]
    return cos.reshape(n, n), -sin.reshape(n, n)


def kernel(x_prompt, x_sample, norm1_g, w_in, a_v_g, a_ws, a_bs, c_lam_re, c_lam_im, c_log_step, c_b_re, c_b_im,
           c_c_re, c_c_im, c_d, c_w_glu, c_b_glu, out_norm_g, w_out, norm2_g, w_gate, w_up, w_down, final_g):
    seq, d = x_prompt.shape[1:]
    assert x_sample.shape[1:] == (seq, d)
    nb = x_prompt.shape[0] + x_sample.shape[0]
    n_prompt = x_prompt.shape[0] * seq
    depth = w_in.shape[0]
    xs = (x_prompt.reshape(-1, d), x_sample.reshape(-1, d))

    cs, ssn = _dft_matrices(seq)
    cs, ssn = cs.astype(BF16), ssn.astype(BF16)
    cd, sdn = _dft_matrices(B_HEAD_DIM)
    cds = jnp.concatenate([cd, -sdn], axis=1).astype(BF16)
    row = lambda v: v.reshape(1, -1)

    for l in range(depth):
        og = out_norm_g[l]
        ws = a_ws[l].astype(BF16)
        bs = jnp.broadcast_to(a_bs[l][:, :, None], (A_HEADS, CHUNK, A_HEAD_DIM))
        ya, xc, xsin, zc = _in_call(xs, row(norm1_g[l]), w_in[l].astype(BF16), row(a_v_g[l]), ws, bs,
                                    row(og[:A_WIDTH]), cds)
        yb = _dft_call(cs, ssn, xc, xsin, row(og[A_WIDTH:A_WIDTH + B_WIDTH]), seq)
        s5w = _s5_weights(c_lam_re[l], c_lam_im[l], c_log_step[l], c_b_re[l], c_b_im[l], c_c_re[l], c_c_im[l],
                          c_d[l])
        yc = _s5_mixer(zc, nb, seq, s5w)
        x = _out_call(xs, ya, yb, yc, c_w_glu[l].astype(BF16), row(c_b_glu[l]), row(og[A_WIDTH + B_WIDTH:]),
                      w_out[l].astype(BF16))
        ffn = functools.partial(_ffn_call, x, row(norm2_g[l]), w_gate[l].astype(BF16), w_up[l].astype(BF16),
                                w_down[l].astype(BF16), row(final_g))
        if l < depth - 1:
            xs = (ffn(final_norm=False),)
        else:
            y_prompt = ffn(final_norm=True, row_start=0, n_rows=n_prompt)
            y_sample = ffn(final_norm=True, row_start=n_prompt, n_rows=x.shape[0] - n_prompt)

    return (y_prompt.reshape(x_prompt.shape), y_sample.reshape(x_sample.shape))
```

```python
import functools
import math

import jax
import jax.numpy as jnp
from jax import lax
from jax.experimental import pallas as pl
from jax.experimental.pallas import tpu as pltpu

F32 = jnp.float32
BF16 = jnp.bfloat16

EPS = 1e-6
A_HEADS = 6
A_HEAD_DIM = 128
A_WIDTH = A_HEADS * A_HEAD_DIM
CHUNK = 128
B_HEADS = 4
B_HEAD_DIM = 128
B_WIDTH = B_HEADS * B_HEAD_DIM
C_GROUP = 16
C_STATE = 64

LANES = 128
SUBLANES = 8
S5_CHUNK = 16
S5_COLS = S5_CHUNK * C_GROUP
VMEM_LIMIT_BYTES = 56 * 1024 * 1024

TOKEN_TILE = 512
FFN_TOKEN_TILE = 1024
FF_TILE = 512
DFT_ROW_TILE = 512
GELU_C = math.sqrt(2.0 / math.pi)


def _gelu(x):
    return 0.5 * x * (1.0 + jnp.tanh(GELU_C * (x + 0.044715 * (x * x * x))))


def _sigmoid(x):
    return 1.0 / (1.0 + jnp.exp(-x))


def _rms(x, g):
    return x * lax.rsqrt(jnp.mean(x * x, axis=-1, keepdims=True) + EPS) * g


def _dot(a, b):
    return jnp.dot(a, b, preferred_element_type=F32)


def _const_spec(shape, layer=None):
    nd = len(shape)
    if layer is None:
        return pl.BlockSpec(shape, lambda *_: (0,) * nd, pipeline_mode=pl.Buffered(1))
    return pl.BlockSpec((None,) + tuple(shape), lambda *_: (layer,) + (0,) * nd, pipeline_mode=pl.Buffered(1))


def _row_tile_specs(xs, tm):
    specs, bounds, start = [], [], 0
    for x in xs:
        n = x.shape[0] // tm
        assert n * tm == x.shape[0] and tm % CHUNK == 0, (x.shape, tm)
        specs.append(pl.BlockSpec((tm, x.shape[1]),
                                  lambda i, *_, start=start, n=n: (jnp.clip(i - start, 0, n - 1), 0)))
        start += n
        bounds.append(start)
    return specs, tuple(bounds)


def _select_rows(x_refs, bounds):
    x = x_refs[-1][...]
    for k in range(len(x_refs) - 2, -1, -1):
        x = jnp.where(pl.program_id(0) < bounds[k], x_refs[k][...], x)
    return x


def _params(*sem):
    return pltpu.CompilerParams(dimension_semantics=sem, vmem_limit_bytes=VMEM_LIMIT_BYTES)


def _in_body(*refs, tile_bounds):
    nx = len(tile_bounds)
    x_refs = refs[:nx]
    (g1_ref, w_ref, vg_ref, ws_ref, bs_ref, oga_ref, cds_ref,
     ya_ref, xc_ref, xs_ref, zc_ref, z0_scr, z1_scr, ya_scr) = refs[nx:]
    tm = ya_ref.shape[0]
    nch = tm // CHUNK
    i = pl.program_id(0)

    def step(z_new, z_old):
        h = _rms(_select_rows(x_refs, tile_bounds), g1_ref[...]).astype(BF16)
        zb = _dot(h, w_ref[:, 2 * A_WIDTH:2 * A_WIDTH + B_WIDTH]).astype(BF16)
        slab = 2 * A_WIDTH // A_HEADS
        for hd in range(A_HEADS):
            z_new[:, hd * slab:(hd + 1) * slab] = _dot(h, w_ref[:, hd * slab:(hd + 1) * slab])
            lo = hd * A_HEAD_DIM
            u = _gelu(z_old[:, lo:lo + A_HEAD_DIM])
            v = _gelu(z_old[:, A_WIDTH + lo:A_WIDTH + lo + A_HEAD_DIM])
            v = _rms(v, vg_ref[:, lo:lo + A_HEAD_DIM]).astype(BF16)
            v_wide = jnp.concatenate([v[c * CHUNK:(c + 1) * CHUNK, :] for c in range(nch)], axis=1)
            mixed = _dot(ws_ref[hd], v_wide)
            for c in range(nch):
                r = c * CHUNK
                ya_scr[r:r + CHUNK, lo:lo + A_HEAD_DIM] = u[r:r + CHUNK, :] * (
                    mixed[:, c * A_HEAD_DIM:(c + 1) * A_HEAD_DIM] + bs_ref[hd])
        zc_ref[...] = _dot(h, w_ref[:, 2 * A_WIDTH + B_WIDTH:])
        for g in range(B_HEADS):
            lo = g * B_HEAD_DIM
            t = _dot(zb[:, lo:lo + B_HEAD_DIM], cds_ref[...])
            xc_ref[:, lo:lo + B_HEAD_DIM] = t[:, :B_HEAD_DIM].astype(BF16)
            xs_ref[:, lo:lo + B_HEAD_DIM] = t[:, B_HEAD_DIM:].astype(BF16)
        ya_ref[...] = _rms(ya_scr[...], oga_ref[...]).astype(BF16)

    @pl.when(i == 0)
    def _():
        z1_scr[...] = jnp.zeros(z1_scr.shape, F32)

    @pl.when(i % 2 == 0)
    def _():
        step(z0_scr, z1_scr)

    @pl.when(i % 2 == 1)
    def _():
        step(z1_scr, z0_scr)


def _in_call(xs, g1, w_in, layer, vg, ws, bs, oga, cds):
    d = xs[0].shape[1]
    n_in = w_in.shape[2]
    c_width = n_in - 2 * A_WIDTH - B_WIDTH
    tm = TOKEN_TILE
    x_specs, tile_bounds = _row_tile_specs(xs, tm)
    nt = tile_bounds[-1]
    t = nt * tm
    this_tile = lambda w: pl.BlockSpec((tm, w), lambda i: (jnp.minimum(i, nt - 1), 0))
    prev_tile = lambda w: pl.BlockSpec((tm, w), lambda i: (jnp.maximum(i - 1, 0), 0))
    return pl.pallas_call(
        functools.partial(_in_body, tile_bounds=tile_bounds),
        grid=(nt + 1,),
        in_specs=x_specs + [_const_spec((1, d)), _const_spec((d, n_in), layer), _const_spec((1, A_WIDTH)),
                            _const_spec(ws.shape), _const_spec(bs.shape), _const_spec((1, A_WIDTH)),
                            _const_spec(cds.shape)],
        out_specs=[prev_tile(A_WIDTH), this_tile(B_WIDTH), this_tile(B_WIDTH), this_tile(c_width)],
        out_shape=[jax.ShapeDtypeStruct((t, A_WIDTH), BF16), jax.ShapeDtypeStruct((t, B_WIDTH), BF16),
                   jax.ShapeDtypeStruct((t, B_WIDTH), BF16), jax.ShapeDtypeStruct((t, c_width), F32)],
        scratch_shapes=[pltpu.VMEM((tm, 2 * A_WIDTH), F32), pltpu.VMEM((tm, 2 * A_WIDTH), F32),
                        pltpu.VMEM((tm, A_WIDTH), F32)],
        compiler_params=_params("arbitrary"),
        name="in_proj_gmlp",
    )(*xs, g1, w_in, vg, ws, bs, oga, cds)


def _dft_body(cs_ref, ssn_ref, xc_ref, xs_ref, og_ref, o_ref, *, scale):
    y = _dot(cs_ref[...], xc_ref[...]) + _dot(ssn_ref[...], xs_ref[...])
    o_ref[...] = _rms(y * scale, og_ref[...]).astype(BF16)


def _dft_call(cs, ssn, xc, xs, ogb, seq):
    t, w = xc.shape
    nb = t // seq
    tk = min(DFT_ROW_TILE, seq)
    nk = seq // tk
    scale = 1.0 / math.sqrt(seq * B_HEAD_DIM)
    lhs = pl.BlockSpec((tk, seq), lambda k, b: (k, 0))
    rhs = pl.BlockSpec((seq, w), lambda k, b: (b, 0))
    return pl.pallas_call(
        functools.partial(_dft_body, scale=scale),
        grid=(nk, nb),
        in_specs=[lhs, lhs, rhs, rhs, pl.BlockSpec((1, w), lambda k, b: (0, 0))],
        out_specs=pl.BlockSpec((tk, w), lambda k, b: (b * nk + k, 0)),
        out_shape=jax.ShapeDtypeStruct((t, w), BF16),
        compiler_params=_params("parallel", "parallel"),
        name="fnet_seq_dft",
    )(cs, ssn, xc, xs, ogb)


def _s5_weights(lam_re, lam_im, log_step, b_re, b_im, c_re, c_im, d_skip):
    L = S5_CHUNK
    G, P = lam_re.shape[1:]
    step = jnp.exp(log_step)[..., None]
    lr, li = lam_re, lam_im
    mag = jnp.exp(lr * step)
    ab_re, ab_im = mag * jnp.cos(li * step), mag * jnp.sin(li * step)
    den = lr * lr + li * li
    nr = ab_re - 1.0
    q_re = (nr * lr + ab_im * li) / den
    q_im = (ab_im * lr - nr * li) / den
    bb_re = q_re[..., None] * b_re[None] - q_im[..., None] * b_im[None]
    bb_im = q_re[..., None] * b_im[None] + q_im[..., None] * b_re[None]
    k = jnp.arange(L + 1, dtype=F32)[:, None, None, None]
    pmag = jnp.exp(k * (lr * step))
    pw_re, pw_im = pmag * jnp.cos(k * (li * step)), pmag * jnp.sin(k * (li * step))
    w_re = pw_re[..., None] * bb_re - pw_im[..., None] * bb_im
    w_im = pw_re[..., None] * bb_im + pw_im[..., None] * bb_re
    kern = (jnp.einsum('gop,ktgpc->ktgoc', c_re, w_re[:L]) - jnp.einsum('gop,ktgpc->ktgoc', c_im, w_im[:L]))
    centre = kern[0, 0] + kern[0, 1] + jnp.eye(C_GROUP) * d_skip.reshape(G, 1, C_GROUP)
    lags = jnp.concatenate([kern[1:, 1][::-1], centre[None], kern[1:, 0]], axis=0)
    lags = lags.transpose(1, 3, 0, 2).reshape(G, C_GROUP, (2 * L - 1) * C_GROUP)
    span = (2 * L - 1) * C_GROUP
    tiled = jnp.tile(jnp.pad(lags, ((0, 0), (0, 0), (0, C_GROUP))), (1, 1, L))[:, :, :L * span]
    m = tiled.reshape(G, C_GROUP, L, span)[:, :, :, (L - 1) * C_GROUP:]
    m = m.transpose(0, 2, 1, 3).reshape(G, S5_COLS, S5_COLS)
    to_rows = lambda a: a.transpose(1, 0, 3, 2).reshape(G, S5_COLS, P)
    p_mat = jnp.concatenate([to_rows(w_re[:L, 0][::-1]), to_rows(w_re[:L, 1]),
                             to_rows(w_im[:L, 0][::-1]), to_rows(w_im[:L, 1])], axis=-1)
    def readout(pr, pi):
        wr = c_re[None] * pr[:, :, None, :] - c_im[None] * pi[:, :, None, :]
        wi = c_re[None] * pi[:, :, None, :] + c_im[None] * pr[:, :, None, :]
        fl = lambda a: a.transpose(1, 3, 0, 2).reshape(G, P, S5_COLS)
        return fl(wr), fl(-wi)
    qf_re, qf_im = readout(pw_re[1:, 0], pw_im[1:, 0])
    qb_re, qb_im = readout(pw_re[1:, 1][::-1], pw_im[1:, 1][::-1])
    z = jnp.zeros_like(qf_re)
    qf = jnp.concatenate([qf_re, z, qf_im, z], axis=1)
    qb = jnp.concatenate([z, qb_re, z, qb_im], axis=1)
    w_all = jnp.concatenate([m, qf, qb], axis=1)
    a_re = jnp.concatenate([pw_re[L, 0], pw_re[L, 1]], axis=-1)
    a_im = jnp.concatenate([pw_im[L, 0], pw_im[L, 1]], axis=-1)
    return p_mat.astype(BF16), w_all.astype(BF16), a_re, a_im


GROUPS_PER_BLOCK = LANES // C_GROUP


def _s5_body(z_ref, p_ref, w_ref, are_ref, aim_ref, o_ref, v_scr, u_scr, s_scr, ha_scr, hb_scr, y_scr):
    nchunks = z_ref.shape[1] // S5_CHUNK
    rows = GROUPS_PER_BLOCK
    n = nchunks * rows
    for i in range(S5_CHUNK):
        t = z_ref[0, pl.ds(i, nchunks, stride=S5_CHUNK), :].T
        for g in range(rows):
            v_scr[g, i * C_GROUP:(i + 1) * C_GROUP, :] = t[g * C_GROUP:(g + 1) * C_GROUP, :]
    for g in range(rows):
        u = v_scr[g].T.astype(BF16)
        u_scr[g] = u
        s = _dot(u, p_ref[g])
        for part in range(2):
            s_scr[part, pl.ds(g, nchunks, stride=rows), :] = s[:, part * LANES:(part + 1) * LANES]

    ar = are_ref[...]
    ai = aim_ref[...]
    is_fwd = lax.broadcasted_iota(jnp.int32, (rows, LANES), 1) < C_STATE
    zero = jnp.zeros((rows, LANES), F32)
    for part in range(2):
        ha_scr[part, 0:rows, :] = zero
        hb_scr[part, n - rows:n, :] = zero

    def step(c, carry):
        hre, him = carry
        r_f = pl.multiple_of(c * rows, rows)
        r_b = pl.multiple_of((nchunks - 1 - c) * rows, rows)
        sre = jnp.where(is_fwd, s_scr[0, pl.ds(r_f, rows), :], s_scr[0, pl.ds(r_b, rows), :])
        sim = jnp.where(is_fwd, s_scr[1, pl.ds(r_f, rows), :], s_scr[1, pl.ds(r_b, rows), :])
        nre = hre * ar - him * ai + sre
        nim = hre * ai + him * ar + sim
        for part, h in enumerate((nre, nim)):
            ha_scr[part, pl.ds(pl.multiple_of(r_f + rows, rows), rows), :] = h
            hb_scr[part, pl.ds(pl.multiple_of(r_b - rows, rows), rows), :] = h
        return nre, nim

    lax.fori_loop(0, nchunks - 1, step, (zero, zero))

    def group_rows(scr, g):
        return jnp.concatenate([scr[part, pl.ds(g, nchunks, stride=rows), :] for part in range(2)],
                               axis=1).astype(BF16)

    for g in range(rows):
        ha = group_rows(ha_scr, g)
        hb = group_rows(hb_scr, g)
        y = (_dot(u_scr[g], w_ref[g, 0:S5_COLS, :]) + _dot(ha, w_ref[g, S5_COLS:2 * S5_COLS, :])
             + _dot(hb, w_ref[g, 2 * S5_COLS:3 * S5_COLS, :]))
        t = y.T
        for j in range(S5_CHUNK):
            y_scr[j, g * C_GROUP:(g + 1) * C_GROUP, :] = t[j * C_GROUP:(j + 1) * C_GROUP, :]
    for j in range(S5_CHUNK):
        o_ref[0, pl.ds(j, nchunks, stride=S5_CHUNK), :] = y_scr[j].T


def _s5_mixer(zc, nb, seq, weights):
    p_mat, w_all, a_re, a_im = weights
    width = zc.shape[1]
    nchunks = seq // S5_CHUNK
    gb = GROUPS_PER_BLOCK
    per_block = lambda a: pl.BlockSpec((gb,) + a.shape[1:], lambda b, n: (b,) + (0,) * (a.ndim - 1))
    seq_block = pl.BlockSpec((1, seq, LANES), lambda b, n: (n, 0, b))
    return pl.pallas_call(
        _s5_body,
        grid=(width // LANES, nb),
        in_specs=[seq_block, per_block(p_mat), per_block(w_all), per_block(a_re), per_block(a_im)],
        out_specs=seq_block,
        out_shape=jax.ShapeDtypeStruct((nb, seq, width), F32),
        scratch_shapes=[pltpu.VMEM((gb, S5_COLS, nchunks), F32), pltpu.VMEM((gb, nchunks, S5_COLS), BF16),
                        pltpu.VMEM((2, nchunks * gb, LANES), F32), pltpu.VMEM((2, nchunks * gb, LANES), F32),
                        pltpu.VMEM((2, nchunks * gb, LANES), F32), pltpu.VMEM((S5_CHUNK, LANES, nchunks), F32)],
        compiler_params=_params("parallel", "parallel"),
        name="s5_mixer",
    )(zc.reshape(nb, seq, width), p_mat, w_all, a_re, a_im).reshape(nb * seq, width)


def _out_body(*refs, tile_bounds):
    nx = len(tile_bounds)
    x_refs = refs[:nx]
    ya_ref, yb_ref, yc_ref, wg_ref, bg_ref, ogc_ref, wo_ref, o_ref = refs[nx:]
    y = _gelu(yc_ref[...].astype(F32))
    yc = y * _sigmoid(_dot(y.astype(BF16), wg_ref[...]) + bg_ref[...])
    yc = _rms(yc, ogc_ref[...]).astype(BF16)
    acc = _select_rows(x_refs, tile_bounds) + _dot(ya_ref[...], wo_ref[0:A_WIDTH, :])
    acc = acc + _dot(yb_ref[...], wo_ref[A_WIDTH:A_WIDTH + B_WIDTH, :])
    o_ref[...] = acc + _dot(yc, wo_ref[A_WIDTH + B_WIDTH:, :])


def _out_call(xs, ya, yb, yc, w_glu, b_glu, ogc, w_out, layer):
    t, cw = yc.shape
    d = w_out.shape[2]
    tm = TOKEN_TILE
    x_specs, tile_bounds = _row_tile_specs(xs, tm)
    assert tile_bounds[-1] * tm == t
    row = lambda w: pl.BlockSpec((tm, w), lambda i: (i, 0))
    return pl.pallas_call(
        functools.partial(_out_body, tile_bounds=tile_bounds),
        grid=(t // tm,),
        in_specs=x_specs + [row(A_WIDTH), row(B_WIDTH), row(cw), _const_spec((cw, cw), layer),
                            _const_spec((1, cw)), _const_spec((1, cw)), _const_spec(w_out.shape[1:], layer)],
        out_specs=row(d),
        out_shape=jax.ShapeDtypeStruct((t, d), F32),
        compiler_params=_params("parallel"),
        name="glu_out_proj",
    )(*xs, ya, yb, yc, w_glu, b_glu, ogc, w_out)


def _ffn_body(x_ref, g2_ref, wg_ref, wu_ref, wd_ref, gf_ref, o_ref, h_scr, *, final_norm):
    f = pl.program_id(1)

    @pl.when(f == 0)
    def _():
        x = x_ref[...]
        h_scr[...] = _rms(x, g2_ref[...]).astype(BF16)
        o_ref[...] = x

    h = h_scr[...]
    gate = _dot(h, wg_ref[...])
    a = (gate * _sigmoid(gate) * _dot(h, wu_ref[...])).astype(BF16)
    o_ref[...] += _dot(a, wd_ref[...])

    if final_norm:
        @pl.when(f == pl.num_programs(1) - 1)
        def _():
            o_ref[...] = _rms(o_ref[...], gf_ref[...])


def _ffn_call(x, g2, w_gate, w_up, w_down, layer, gf, final_norm, row_start=0, n_rows=None):
    d = x.shape[1]
    ff = w_gate.shape[2]
    n_rows = x.shape[0] if n_rows is None else n_rows
    tm = min(FFN_TOKEN_TILE, n_rows)
    tf = FF_TILE
    assert n_rows % tm == 0 and row_start % tm == 0 and ff % tf == 0
    first = row_start // tm
    return pl.pallas_call(
        functools.partial(_ffn_body, final_norm=final_norm),
        grid=(n_rows // tm, ff // tf),
        in_specs=[pl.BlockSpec((tm, d), lambda i, f: (i + first, 0)), _const_spec((1, d)),
                  pl.BlockSpec((None, d, tf), lambda i, f: (layer, 0, f)),
                  pl.BlockSpec((None, d, tf), lambda i, f: (layer, 0, f)),
                  pl.BlockSpec((None, tf, d), lambda i, f: (layer, f, 0)), _const_spec((1, d))],
        out_specs=pl.BlockSpec((tm, d), lambda i, f: (i, 0)),
        out_shape=jax.ShapeDtypeStruct((n_rows, d), F32),
        scratch_shapes=[pltpu.VMEM((tm, d), BF16)],
        compiler_params=_params("parallel", "arbitrary"),
        name="swiglu_ffn",
    )(x, g2, w_gate, w_up, w_down, gf)


def _dft_matrices(n):
    s = jnp.arange(n, dtype=jnp.int32)[None, :]
    unit = 2.0 * math.pi / n
    r = math.isqrt(n)
    if r * r != n:
        ang = ((jnp.arange(n, dtype=jnp.int32)[:, None] * s) % n).astype(F32) * unit
        return jnp.cos(ang), -jnp.sin(ang)
    ab = jnp.arange(r, dtype=jnp.int32)[:, None]
    alpha = (((r * ab) * s) % n).astype(F32) * unit
    beta = ((ab * s) % n).astype(F32) * unit
    ca, sa, cb, sb = jnp.cos(alpha), jnp.sin(alpha), jnp.cos(beta), jnp.sin(beta)
    cos = ca[:, None, :] * cb[None, :, :] - sa[:, None, :] * sb[None, :, :]
    sin = sa[:, None, :] * cb[None, :, :] + ca[:, None, :] * sb[None, :, :]
    return cos.reshape(n, n), -sin.reshape(n, n)


def kernel(x_prompt, x_sample, norm1_g, w_in, a_v_g, a_ws, a_bs, c_lam_re, c_lam_im, c_log_step, c_b_re, c_b_im,
           c_c_re, c_c_im, c_d, c_w_glu, c_b_glu, out_norm_g, w_out, norm2_g, w_gate, w_up, w_down, final_g):
    seq, d = x_prompt.shape[1:]
    assert x_sample.shape[1:] == (seq, d)
    nb = x_prompt.shape[0] + x_sample.shape[0]
    n_prompt = x_prompt.shape[0] * seq
    depth = w_in.shape[0]
    xs = (x_prompt.reshape(-1, d), x_sample.reshape(-1, d))

    cs, ssn = _dft_matrices(seq)
    cs, ssn = cs.astype(BF16), ssn.astype(BF16)
    cd, sdn = _dft_matrices(B_HEAD_DIM)
    cds = jnp.concatenate([cd, -sdn], axis=1).astype(BF16)
    row = lambda v: v.reshape(1, -1)
    w_in, w_out, c_w_glu, w_gate, w_up, w_down = (
        w.astype(BF16) for w in (w_in, w_out, c_w_glu, w_gate, w_up, w_down))

    for l in range(depth):
        og = out_norm_g[l]
        ws = a_ws[l].astype(BF16)
        bs = jnp.broadcast_to(a_bs[l][:, :, None], (A_HEADS, CHUNK, A_HEAD_DIM))
        ya, xc, xsin, zc = _in_call(xs, row(norm1_g[l]), w_in, l, row(a_v_g[l]), ws, bs,
                                    row(og[:A_WIDTH]), cds)
        yb = _dft_call(cs, ssn, xc, xsin, row(og[A_WIDTH:A_WIDTH + B_WIDTH]), seq)
        s5w = _s5_weights(c_lam_re[l], c_lam_im[l], c_log_step[l], c_b_re[l], c_b_im[l], c_c_re[l], c_c_im[l],
                          c_d[l])
        yc = _s5_mixer(zc, nb, seq, s5w)
        x = _out_call(xs, ya, yb, yc, c_w_glu, row(c_b_glu[l]), row(og[A_WIDTH + B_WIDTH:]), w_out, l)
        ffn = functools.partial(_ffn_call, x, row(norm2_g[l]), w_gate, w_up, w_down, l, row(final_g))
        if l < depth - 1:
            xs = (ffn(final_norm=False),)
        else:
            y_prompt = ffn(final_norm=True, row_start=0, n_rows=n_prompt)
            y_sample = ffn(final_norm=True, row_start=n_prompt, n_rows=x.shape[0] - n_prompt)

    return (y_prompt.reshape(x_prompt.shape), y_sample.reshape(x_sample.shape))
```

```python
import functools
import math

import jax
import jax.numpy as jnp
from jax import lax
from jax.experimental import pallas as pl
from jax.experimental.pallas import tpu as pltpu

F32 = jnp.float32
BF16 = jnp.bfloat16

EPS = 1e-6
A_HEADS = 6
A_HEAD_DIM = 128
A_WIDTH = A_HEADS * A_HEAD_DIM
CHUNK = 128
B_HEADS = 4
B_HEAD_DIM = 128
B_WIDTH = B_HEADS * B_HEAD_DIM
C_GROUP = 16
C_STATE = 64

LANES = 128
SUBLANES = 8
S5_CHUNK = 16
S5_COLS = S5_CHUNK * C_GROUP
VMEM_LIMIT_BYTES = 56 * 1024 * 1024

TOKEN_TILE = 512
FFN_TOKEN_TILE = 1024
FF_TILE = 512
DFT_ROW_TILE = 512
GELU_C = math.sqrt(2.0 / math.pi)


def _gelu(x):
    return 0.5 * x * (1.0 + jnp.tanh(GELU_C * (x + 0.044715 * (x * x * x))))


def _sigmoid(x):
    return 1.0 / (1.0 + jnp.exp(-x))


def _rms(x, g):
    return x * lax.rsqrt(jnp.mean(x * x, axis=-1, keepdims=True) + EPS) * g


def _dot(a, b):
    return jnp.dot(a, b, preferred_element_type=F32)


def _const_spec(shape, layer=None):
    nd = len(shape)
    if layer is None:
        return pl.BlockSpec(shape, lambda *_: (0,) * nd, pipeline_mode=pl.Buffered(1))
    return pl.BlockSpec((None,) + tuple(shape), lambda *_: (layer,) + (0,) * nd, pipeline_mode=pl.Buffered(1))


def _row_tile_specs(xs, tm):
    specs, bounds, start = [], [], 0
    for x in xs:
        n = x.shape[0] // tm
        assert n * tm == x.shape[0] and tm % CHUNK == 0, (x.shape, tm)
        specs.append(pl.BlockSpec((tm, x.shape[1]),
                                  lambda i, *_, start=start, n=n: (jnp.clip(i - start, 0, n - 1), 0)))
        start += n
        bounds.append(start)
    return specs, tuple(bounds)


def _select_rows(x_refs, bounds):
    x = x_refs[-1][...]
    for k in range(len(x_refs) - 2, -1, -1):
        x = jnp.where(pl.program_id(0) < bounds[k], x_refs[k][...], x)
    return x


def _params(*sem):
    return pltpu.CompilerParams(dimension_semantics=sem, vmem_limit_bytes=VMEM_LIMIT_BYTES)


def _in_body(*refs, tile_bounds):
    nx = len(tile_bounds)
    x_refs = refs[:nx]
    (g1_ref, w_ref, vg_ref, ws_ref, bs_ref, oga_ref, cds_ref,
     ya_ref, xc_ref, xs_ref, zc_ref, z0_scr, z1_scr, ya_scr) = refs[nx:]
    tm = ya_ref.shape[0]
    nch = tm // CHUNK
    i = pl.program_id(0)

    def step(z_new, z_old):
        h = _rms(_select_rows(x_refs, tile_bounds), g1_ref[...]).astype(BF16)
        zb = _dot(h, w_ref[:, 2 * A_WIDTH:2 * A_WIDTH + B_WIDTH]).astype(BF16)
        slab = 2 * A_WIDTH // A_HEADS
        for hd in range(A_HEADS):
            z_new[:, hd * slab:(hd + 1) * slab] = _dot(h, w_ref[:, hd * slab:(hd + 1) * slab])
            lo = hd * A_HEAD_DIM
            u = _gelu(z_old[:, lo:lo + A_HEAD_DIM])
            v = _gelu(z_old[:, A_WIDTH + lo:A_WIDTH + lo + A_HEAD_DIM])
            v = _rms(v, vg_ref[:, lo:lo + A_HEAD_DIM]).astype(BF16)
            v_wide = jnp.concatenate([v[c * CHUNK:(c + 1) * CHUNK, :] for c in range(nch)], axis=1)
            mixed = _dot(ws_ref[hd], v_wide)
            for c in range(nch):
                r = c * CHUNK
                ya_scr[r:r + CHUNK, lo:lo + A_HEAD_DIM] = u[r:r + CHUNK, :] * (
                    mixed[:, c * A_HEAD_DIM:(c + 1) * A_HEAD_DIM] + bs_ref[hd])
        zc_ref[...] = _dot(h, w_ref[:, 2 * A_WIDTH + B_WIDTH:])
        for g in range(B_HEADS):
            lo = g * B_HEAD_DIM
            t = _dot(zb[:, lo:lo + B_HEAD_DIM], cds_ref[...])
            xc_ref[:, lo:lo + B_HEAD_DIM] = t[:, :B_HEAD_DIM].astype(BF16)
            xs_ref[:, lo:lo + B_HEAD_DIM] = t[:, B_HEAD_DIM:].astype(BF16)
        ya_ref[...] = _rms(ya_scr[...], oga_ref[...]).astype(BF16)

    @pl.when(i == 0)
    def _():
        z1_scr[...] = jnp.zeros(z1_scr.shape, F32)

    @pl.when(i % 2 == 0)
    def _():
        step(z0_scr, z1_scr)

    @pl.when(i % 2 == 1)
    def _():
        step(z1_scr, z0_scr)


def _in_call(xs, g1, w_in, layer, vg, ws, bs, oga, cds):
    d = xs[0].shape[1]
    n_in = w_in.shape[2]
    c_width = n_in - 2 * A_WIDTH - B_WIDTH
    tm = TOKEN_TILE
    x_specs, tile_bounds = _row_tile_specs(xs, tm)
    nt = tile_bounds[-1]
    t = nt * tm
    this_tile = lambda w: pl.BlockSpec((tm, w), lambda i: (jnp.minimum(i, nt - 1), 0))
    prev_tile = lambda w: pl.BlockSpec((tm, w), lambda i: (jnp.maximum(i - 1, 0), 0))
    return pl.pallas_call(
        functools.partial(_in_body, tile_bounds=tile_bounds),
        grid=(nt + 1,),
        in_specs=x_specs + [_const_spec((1, d)), _const_spec((d, n_in), layer), _const_spec((1, A_WIDTH)),
                            _const_spec(ws.shape[1:], layer), _const_spec(bs.shape[1:], layer),
                            _const_spec((1, A_WIDTH)),
                            _const_spec(cds.shape)],
        out_specs=[prev_tile(A_WIDTH), this_tile(B_WIDTH), this_tile(B_WIDTH), this_tile(c_width)],
        out_shape=[jax.ShapeDtypeStruct((t, A_WIDTH), BF16), jax.ShapeDtypeStruct((t, B_WIDTH), BF16),
                   jax.ShapeDtypeStruct((t, B_WIDTH), BF16), jax.ShapeDtypeStruct((t, c_width), F32)],
        scratch_shapes=[pltpu.VMEM((tm, 2 * A_WIDTH), F32), pltpu.VMEM((tm, 2 * A_WIDTH), F32),
                        pltpu.VMEM((tm, A_WIDTH), F32)],
        compiler_params=_params("arbitrary"),
        name="in_proj_gmlp",
    )(*xs, g1, w_in, vg, ws, bs, oga, cds)


DFT_HALO = 16


def _dft_body(c_ref, sn_ref, xc_ref, xs_ref, og_ref, rev_ref, o_ref, *, scale):
    tk = o_ref.shape[1]
    a = _dot(c_ref[0], xc_ref[...])
    bn = _dot(sn_ref[0], xs_ref[...])
    o_ref[0] = _rms((a + bn)[:tk] * scale, og_ref[...]).astype(BF16)
    mirrored = _rms((a - bn) * scale, og_ref[...]).astype(BF16)
    o_ref[1] = _dot(rev_ref[...], mirrored).astype(BF16)


def _dft_call(c_half, sn_half, rev, xc, xs, ogb, seq):
    t, w = xc.shape
    nb = t // seq
    nk, rows, _ = c_half.shape
    tk = rows - DFT_HALO
    assert 2 * nk * tk == seq
    scale = 1.0 / math.sqrt(seq * B_HEAD_DIM)
    lhs = pl.BlockSpec((1, rows, seq), lambda k, b: (k, 0, 0))
    rhs = pl.BlockSpec((seq, w), lambda k, b: (b, 0))
    return pl.pallas_call(
        functools.partial(_dft_body, scale=scale),
        grid=(nk, nb),
        in_specs=[lhs, lhs, rhs, rhs, pl.BlockSpec((1, w), lambda k, b: (0, 0)), _const_spec(rev.shape)],
        out_specs=pl.BlockSpec((None, 2, None, tk, w), lambda k, b: (b, 0, k, 0, 0)),
        out_shape=jax.ShapeDtypeStruct((nb, 2, nk, tk, w), BF16),
        compiler_params=_params("parallel", "parallel"),
        name="fnet_seq_dft",
    )(c_half, sn_half, xc, xs, ogb, rev)


def _dft_half_matrices(n, tk):
    nk = n // (2 * tk)
    rows = tk + DFT_HALO
    r = 64
    n_a = -(-((nk - 1) * tk + rows) // r)
    s = jnp.arange(n, dtype=jnp.int32)[None, :]
    unit = 2.0 * math.pi / n
    alpha = (((r * jnp.arange(n_a, dtype=jnp.int32)[:, None]) * s) % n).astype(F32) * unit
    beta = ((jnp.arange(r, dtype=jnp.int32)[:, None] * s) % n).astype(F32) * unit
    ca, sa, cb, sb = jnp.cos(alpha), jnp.sin(alpha), jnp.cos(beta), jnp.sin(beta)
    cos = (ca[:, None, :] * cb[None, :, :] - sa[:, None, :] * sb[None, :, :]).reshape(n_a * r, n).astype(BF16)
    nsin = (-(sa[:, None, :] * cb[None, :, :] + ca[:, None, :] * sb[None, :, :])).reshape(n_a * r, n).astype(BF16)
    tiles = lambda m: jnp.stack([m[j * tk:j * tk + rows] for j in range(nk)], axis=0)
    rev = (jnp.arange(rows)[None, :] == tk - jnp.arange(tk)[:, None]).astype(BF16)
    return tiles(cos), tiles(nsin), rev


def _s5_weights(lam_re, lam_im, log_step, b_re, b_im, c_re, c_im, d_skip):
    L = S5_CHUNK
    G, P = lam_re.shape[1:]
    step = jnp.exp(log_step)[..., None]
    lr, li = lam_re, lam_im
    mag = jnp.exp(lr * step)
    ab_re, ab_im = mag * jnp.cos(li * step), mag * jnp.sin(li * step)
    den = lr * lr + li * li
    nr = ab_re - 1.0
    q_re = (nr * lr + ab_im * li) / den
    q_im = (ab_im * lr - nr * li) / den
    bb_re = q_re[..., None] * b_re[None] - q_im[..., None] * b_im[None]
    bb_im = q_re[..., None] * b_im[None] + q_im[..., None] * b_re[None]
    k = jnp.arange(L + 1, dtype=F32)[:, None, None, None]
    pmag = jnp.exp(k * (lr * step))
    pw_re, pw_im = pmag * jnp.cos(k * (li * step)), pmag * jnp.sin(k * (li * step))
    w_re = pw_re[..., None] * bb_re - pw_im[..., None] * bb_im
    w_im = pw_re[..., None] * bb_im + pw_im[..., None] * bb_re
    kern = (jnp.einsum('gop,ktgpc->ktgoc', c_re, w_re[:L]) - jnp.einsum('gop,ktgpc->ktgoc', c_im, w_im[:L]))
    centre = kern[0, 0] + kern[0, 1] + jnp.eye(C_GROUP) * d_skip.reshape(G, 1, C_GROUP)
    lags = jnp.concatenate([kern[1:, 1][::-1], centre[None], kern[1:, 0]], axis=0)
    lags = lags.transpose(1, 3, 0, 2).reshape(G, C_GROUP, (2 * L - 1) * C_GROUP)
    span = (2 * L - 1) * C_GROUP
    tiled = jnp.tile(jnp.pad(lags, ((0, 0), (0, 0), (0, C_GROUP))), (1, 1, L))[:, :, :L * span]
    m = tiled.reshape(G, C_GROUP, L, span)[:, :, :, (L - 1) * C_GROUP:]
    m = m.transpose(0, 2, 1, 3).reshape(G, S5_COLS, S5_COLS)
    to_rows = lambda a: a.transpose(1, 0, 3, 2).reshape(G, S5_COLS, P)
    p_mat = jnp.concatenate([to_rows(w_re[:L, 0][::-1]), to_rows(w_re[:L, 1]),
                             to_rows(w_im[:L, 0][::-1]), to_rows(w_im[:L, 1])], axis=-1)
    def readout(pr, pi):
        wr = c_re[None] * pr[:, :, None, :] - c_im[None] * pi[:, :, None, :]
        wi = c_re[None] * pi[:, :, None, :] + c_im[None] * pr[:, :, None, :]
        fl = lambda a: a.transpose(1, 3, 0, 2).reshape(G, P, S5_COLS)
        return fl(wr), fl(-wi)
    qf_re, qf_im = readout(pw_re[1:, 0], pw_im[1:, 0])
    qb_re, qb_im = readout(pw_re[1:, 1][::-1], pw_im[1:, 1][::-1])
    z = jnp.zeros_like(qf_re)
    qf = jnp.concatenate([qf_re, z, qf_im, z], axis=1)
    qb = jnp.concatenate([z, qb_re, z, qb_im], axis=1)
    w_all = jnp.concatenate([m, qf, qb], axis=1)
    a_re = jnp.concatenate([pw_re[L, 0], pw_re[L, 1]], axis=-1)
    a_im = jnp.concatenate([pw_im[L, 0], pw_im[L, 1]], axis=-1)
    return p_mat.astype(BF16), w_all.astype(BF16), a_re, a_im


GROUPS_PER_BLOCK = LANES // C_GROUP


def _s5_body(z_ref, p_ref, w_ref, are_ref, aim_ref, o_ref, v_scr, u_scr, s_scr, ha_scr, hb_scr, y_scr):
    nchunks = z_ref.shape[1] // S5_CHUNK
    rows = GROUPS_PER_BLOCK
    n = nchunks * rows
    for i in range(S5_CHUNK):
        t = z_ref[0, pl.ds(i, nchunks, stride=S5_CHUNK), :].T
        for g in range(rows):
            v_scr[g, i * C_GROUP:(i + 1) * C_GROUP, :] = t[g * C_GROUP:(g + 1) * C_GROUP, :]
    for g in range(rows):
        u = v_scr[g].T.astype(BF16)
        u_scr[g] = u
        s = _dot(u, p_ref[g])
        for part in range(2):
            s_scr[part, pl.ds(g, nchunks, stride=rows), :] = s[:, part * LANES:(part + 1) * LANES]

    ar = are_ref[...]
    ai = aim_ref[...]
    is_fwd = lax.broadcasted_iota(jnp.int32, (rows, LANES), 1) < C_STATE
    zero = jnp.zeros((rows, LANES), F32)
    for part in range(2):
        ha_scr[part, 0:rows, :] = zero
        hb_scr[part, n - rows:n, :] = zero

    def step(c, carry):
        hre, him = carry
        r_f = pl.multiple_of(c * rows, rows)
        r_b = pl.multiple_of((nchunks - 1 - c) * rows, rows)
        sre = jnp.where(is_fwd, s_scr[0, pl.ds(r_f, rows), :], s_scr[0, pl.ds(r_b, rows), :])
        sim = jnp.where(is_fwd, s_scr[1, pl.ds(r_f, rows), :], s_scr[1, pl.ds(r_b, rows), :])
        nre = hre * ar - him * ai + sre
        nim = hre * ai + him * ar + sim
        for part, h in enumerate((nre, nim)):
            ha_scr[part, pl.ds(pl.multiple_of(r_f + rows, rows), rows), :] = h
            hb_scr[part, pl.ds(pl.multiple_of(r_b - rows, rows), rows), :] = h
        return nre, nim

    lax.fori_loop(0, nchunks - 1, step, (zero, zero))

    def group_rows(scr, g):
        return jnp.concatenate([scr[part, pl.ds(g, nchunks, stride=rows), :] for part in range(2)],
                               axis=1).astype(BF16)

    for g in range(rows):
        ha = group_rows(ha_scr, g)
        hb = group_rows(hb_scr, g)
        y = (_dot(u_scr[g], w_ref[g, 0:S5_COLS, :]) + _dot(ha, w_ref[g, S5_COLS:2 * S5_COLS, :])
             + _dot(hb, w_ref[g, 2 * S5_COLS:3 * S5_COLS, :]))
        t = y.T
        for j in range(S5_CHUNK):
            y_scr[j, g * C_GROUP:(g + 1) * C_GROUP, :] = t[j * C_GROUP:(j + 1) * C_GROUP, :]
    for j in range(S5_CHUNK):
        o_ref[0, pl.ds(j, nchunks, stride=S5_CHUNK), :] = y_scr[j].T


def _s5_mixer(zc, nb, seq, weights, layer):
    p_mat, w_all, a_re, a_im = weights
    width = zc.shape[1]
    nchunks = seq // S5_CHUNK
    gb = GROUPS_PER_BLOCK
    per_block = lambda a: pl.BlockSpec((None, gb) + a.shape[2:], lambda b, n: (layer, b) + (0,) * (a.ndim - 2))
    seq_block = pl.BlockSpec((1, seq, LANES), lambda b, n: (n, 0, b))
    return pl.pallas_call(
        _s5_body,
        grid=(width // LANES, nb),
        in_specs=[seq_block, per_block(p_mat), per_block(w_all), per_block(a_re), per_block(a_im)],
        out_specs=seq_block,
        out_shape=jax.ShapeDtypeStruct((nb, seq, width), F32),
        scratch_shapes=[pltpu.VMEM((gb, S5_COLS, nchunks), F32), pltpu.VMEM((gb, nchunks, S5_COLS), BF16),
                        pltpu.VMEM((2, nchunks * gb, LANES), F32), pltpu.VMEM((2, nchunks * gb, LANES), F32),
                        pltpu.VMEM((2, nchunks * gb, LANES), F32), pltpu.VMEM((S5_CHUNK, LANES, nchunks), F32)],
        compiler_params=_params("parallel", "parallel"),
        name="s5_mixer",
    )(zc.reshape(nb, seq, width), p_mat, w_all, a_re, a_im).reshape(nb * seq, width)


def _out_body(*refs, tile_bounds):
    nx = len(tile_bounds)
    x_refs = refs[:nx]
    ya_ref, yb_ref, yc_ref, wg_ref, bg_ref, ogc_ref, wo_ref, o_ref = refs[nx:]
    acc = _select_rows(x_refs, tile_bounds) + _dot(ya_ref[...], wo_ref[0:A_WIDTH, :])
    acc = acc + _dot(yb_ref[...], wo_ref[A_WIDTH:A_WIDTH + B_WIDTH, :])
    y = _gelu(yc_ref[...].astype(F32))
    yc = y * _sigmoid(_dot(y.astype(BF16), wg_ref[...]) + bg_ref[...])
    yc = _rms(yc, ogc_ref[...]).astype(BF16)
    o_ref[...] = acc + _dot(yc, wo_ref[A_WIDTH + B_WIDTH:, :])


def _out_call(xs, ya, yb, yc, w_glu, b_glu, ogc, w_out, layer):
    t, cw = yc.shape
    d = w_out.shape[2]
    tm = TOKEN_TILE
    x_specs, tile_bounds = _row_tile_specs(xs, tm)
    assert tile_bounds[-1] * tm == t
    row = lambda w: pl.BlockSpec((tm, w), lambda i: (i, 0))
    _, _, nk, tk, bw = yb.shape
    assert tk == tm

    def yb_index(i):
        j = i % (2 * nk)
        return (i // (2 * nk), j // nk, jnp.where(j < nk, j, 2 * nk - 1 - j), 0, 0)

    return pl.pallas_call(
        functools.partial(_out_body, tile_bounds=tile_bounds),
        grid=(t // tm,),
        in_specs=x_specs + [row(A_WIDTH), pl.BlockSpec((None, None, None, tm, bw), yb_index), row(cw),
                            _const_spec((cw, cw), layer),
                            _const_spec((1, cw)), _const_spec((1, cw)), _const_spec(w_out.shape[1:], layer)],
        out_specs=row(d),
        out_shape=jax.ShapeDtypeStruct((t, d), F32),
        compiler_params=_params("parallel"),
        name="glu_out_proj",
    )(*xs, ya, yb, yc, w_glu, b_glu, ogc, w_out)


def _ffn_body(x_ref, g2_ref, wg_ref, wu_ref, wd_ref, gf_ref, o_ref, h_scr, *, final_norm):
    f = pl.program_id(1)

    @pl.when(f == 0)
    def _():
        x = x_ref[...]
        h_scr[...] = _rms(x, g2_ref[...]).astype(BF16)
        o_ref[...] = x

    h = h_scr[...]
    gate = _dot(h, wg_ref[...])
    a = (gate * _sigmoid(gate) * _dot(h, wu_ref[...])).astype(BF16)
    o_ref[...] += _dot(a, wd_ref[...])

    if final_norm:
        @pl.when(f == pl.num_programs(1) - 1)
        def _():
            o_ref[...] = _rms(o_ref[...], gf_ref[...])


def _ffn_call(x, g2, w_gate, w_up, w_down, layer, gf, final_norm, row_start=0, n_rows=None):
    d = x.shape[1]
    ff = w_gate.shape[2]
    n_rows = x.shape[0] if n_rows is None else n_rows
    tm = min(FFN_TOKEN_TILE, n_rows)
    tf = FF_TILE
    assert n_rows % tm == 0 and row_start % tm == 0 and ff % tf == 0
    first = row_start // tm
    return pl.pallas_call(
        functools.partial(_ffn_body, final_norm=final_norm),
        grid=(n_rows // tm, ff // tf),
        in_specs=[pl.BlockSpec((tm, d), lambda i, f: (i + first, 0)), _const_spec((1, d)),
                  pl.BlockSpec((None, d, tf), lambda i, f: (layer, 0, f)),
                  pl.BlockSpec((None, d, tf), lambda i, f: (layer, 0, f)),
                  pl.BlockSpec((None, tf, d), lambda i, f: (layer, f, 0)), _const_spec((1, d))],
        out_specs=pl.BlockSpec((tm, d), lambda i, f: (i, 0)),
        out_shape=jax.ShapeDtypeStruct((n_rows, d), F32),
        scratch_shapes=[pltpu.VMEM((tm, d), BF16)],
        compiler_params=_params("parallel", "arbitrary"),
        name="swiglu_ffn",
    )(x, g2, w_gate, w_up, w_down, gf)


def _dft_matrices(n):
    s = jnp.arange(n, dtype=jnp.int32)[None, :]
    unit = 2.0 * math.pi / n
    r = math.isqrt(n)
    if r * r != n:
        ang = ((jnp.arange(n, dtype=jnp.int32)[:, None] * s) % n).astype(F32) * unit
        return jnp.cos(ang), -jnp.sin(ang)
    ab = jnp.arange(r, dtype=jnp.int32)[:, None]
    alpha = (((r * ab) * s) % n).astype(F32) * unit
    beta = ((ab * s) % n).astype(F32) * unit
    ca, sa, cb, sb = jnp.cos(alpha), jnp.sin(alpha), jnp.cos(beta), jnp.sin(beta)
    cos = ca[:, None, :] * cb[None, :, :] - sa[:, None, :] * sb[None, :, :]
    sin = sa[:, None, :] * cb[None, :, :] + ca[:, None, :] * sb[None, :, :]
    return cos.reshape(n, n), -sin.reshape(n, n)


def kernel(x_prompt, x_sample, norm1_g, w_in, a_v_g, a_ws, a_bs, c_lam_re, c_lam_im, c_log_step, c_b_re, c_b_im,
           c_c_re, c_c_im, c_d, c_w_glu, c_b_glu, out_norm_g, w_out, norm2_g, w_gate, w_up, w_down, final_g):
    seq, d = x_prompt.shape[1:]
    assert x_sample.shape[1:] == (seq, d)
    nb = x_prompt.shape[0] + x_sample.shape[0]
    n_prompt = x_prompt.shape[0] * seq
    depth = w_in.shape[0]
    xs = (x_prompt.reshape(-1, d), x_sample.reshape(-1, d))

    c_half, sn_half, rev = _dft_half_matrices(seq, min(DFT_ROW_TILE, seq // 2))
    cd, sdn = _dft_matrices(B_HEAD_DIM)
    cds = jnp.concatenate([cd, -sdn], axis=1).astype(BF16)
    row = lambda v: v.reshape(1, -1)
    w_in, w_out, c_w_glu, w_gate, w_up, w_down = (
        w.astype(BF16) for w in (w_in, w_out, c_w_glu, w_gate, w_up, w_down))
    s5w = jax.vmap(_s5_weights)(c_lam_re, c_lam_im, c_log_step, c_b_re, c_b_im, c_c_re, c_c_im, c_d)

    ws = a_ws.astype(BF16)
    bs = jnp.broadcast_to(a_bs[..., None], a_bs.shape + (A_HEAD_DIM,))

    for l in range(depth):
        og = out_norm_g[l]
        ya, xc, xsin, zc = _in_call(xs, row(norm1_g[l]), w_in, l, row(a_v_g[l]), ws, bs,
                                    row(og[:A_WIDTH]), cds)
        yb = _dft_call(c_half, sn_half, rev, xc, xsin, row(og[A_WIDTH:A_WIDTH + B_WIDTH]), seq)
        yc = _s5_mixer(zc, nb, seq, s5w, l)
        x = _out_call(xs, ya, yb, yc, c_w_glu, row(c_b_glu[l]), row(og[A_WIDTH + B_WIDTH:]), w_out, l)
        ffn = functools.partial(_ffn_call, x, row(norm2_g[l]), w_gate, w_up, w_down, l, row(final_g))
        if l < depth - 1:
            xs = (ffn(final_norm=False),)
        else:
            y_prompt = ffn(final_norm=True, row_start=0, n_rows=n_prompt)
            y_sample = ffn(final_norm=True, row_start=n_prompt, n_rows=x.shape[0] - n_prompt)

    return (y_prompt.reshape(x_prompt.shape), y_sample.reshape(x_sample.shape))
```

```python
import functools
import math

import jax
import jax.numpy as jnp
from jax import lax
from jax.experimental import pallas as pl
from jax.experimental.pallas import tpu as pltpu

F32 = jnp.float32
BF16 = jnp.bfloat16

EPS = 1e-6
A_HEADS = 6
A_HEAD_DIM = 128
A_WIDTH = A_HEADS * A_HEAD_DIM
CHUNK = 128
B_HEADS = 4
B_HEAD_DIM = 128
B_WIDTH = B_HEADS * B_HEAD_DIM
C_GROUP = 16
C_STATE = 64

LANES = 128
SUBLANES = 8
S5_CHUNK = 16
S5_COLS = S5_CHUNK * C_GROUP
VMEM_LIMIT_BYTES = 56 * 1024 * 1024

TOKEN_TILE = 512
FFN_TOKEN_TILE = 1024
FF_TILE = 512
DFT_ROW_TILE = 512
GELU_C = math.sqrt(2.0 / math.pi)


def _gelu(x):
    return 0.5 * x * (1.0 + jnp.tanh(GELU_C * (x + 0.044715 * (x * x * x))))


def _sigmoid(x):
    return 1.0 / (1.0 + jnp.exp(-x))


def _rms(x, g):
    return x * lax.rsqrt(jnp.mean(x * x, axis=-1, keepdims=True) + EPS) * g


def _dot(a, b):
    return jnp.dot(a, b, preferred_element_type=F32)


def _const_spec(shape, layer=None):
    nd = len(shape)
    if layer is None:
        return pl.BlockSpec(shape, lambda *_: (0,) * nd, pipeline_mode=pl.Buffered(1))
    return pl.BlockSpec((None,) + tuple(shape), lambda *_: (layer,) + (0,) * nd, pipeline_mode=pl.Buffered(1))


def _row_tile_specs(xs, tm):
    specs, bounds, start = [], [], 0
    for x in xs:
        n = x.shape[0] // tm
        assert n * tm == x.shape[0] and tm % CHUNK == 0, (x.shape, tm)
        specs.append(pl.BlockSpec((tm, x.shape[1]),
                                  lambda i, *_, start=start, n=n: (jnp.clip(i - start, 0, n - 1), 0)))
        start += n
        bounds.append(start)
    return specs, tuple(bounds)


def _select_rows(x_refs, bounds):
    x = x_refs[-1][...]
    for k in range(len(x_refs) - 2, -1, -1):
        x = jnp.where(pl.program_id(0) < bounds[k], x_refs[k][...], x)
    return x


def _params(*sem):
    return pltpu.CompilerParams(dimension_semantics=sem, vmem_limit_bytes=VMEM_LIMIT_BYTES)


def _in_body(*refs, tile_bounds):
    nx = len(tile_bounds)
    x_refs = refs[:nx]
    (g1_ref, w_ref, vg_ref, ws_ref, bs_ref, oga_ref, cds_ref,
     ya_ref, xc_ref, xs_ref, zc_ref, z0_scr, z1_scr, ya_scr) = refs[nx:]
    tm = ya_ref.shape[0]
    nch = tm // CHUNK
    i = pl.program_id(0)

    def step(z_new, z_old):
        h = _rms(_select_rows(x_refs, tile_bounds), g1_ref[...]).astype(BF16)
        zb = _dot(h, w_ref[:, 2 * A_WIDTH:2 * A_WIDTH + B_WIDTH]).astype(BF16)
        slab = 2 * A_WIDTH // A_HEADS
        for hd in range(A_HEADS):
            z_new[:, hd * slab:(hd + 1) * slab] = _dot(h, w_ref[:, hd * slab:(hd + 1) * slab])
            lo = hd * A_HEAD_DIM
            u = _gelu(z_old[:, lo:lo + A_HEAD_DIM])
            v = _gelu(z_old[:, A_WIDTH + lo:A_WIDTH + lo + A_HEAD_DIM])
            v = _rms(v, vg_ref[:, lo:lo + A_HEAD_DIM]).astype(BF16)
            v_wide = jnp.concatenate([v[c * CHUNK:(c + 1) * CHUNK, :] for c in range(nch)], axis=1)
            mixed = _dot(ws_ref[hd], v_wide)
            for c in range(nch):
                r = c * CHUNK
                ya_scr[r:r + CHUNK, lo:lo + A_HEAD_DIM] = u[r:r + CHUNK, :] * (
                    mixed[:, c * A_HEAD_DIM:(c + 1) * A_HEAD_DIM] + bs_ref[hd])
        zc_ref[...] = _dot(h, w_ref[:, 2 * A_WIDTH + B_WIDTH:])
        for g in range(B_HEADS):
            lo = g * B_HEAD_DIM
            t = _dot(zb[:, lo:lo + B_HEAD_DIM], cds_ref[...])
            xc_ref[:, lo:lo + B_HEAD_DIM] = t[:, :B_HEAD_DIM].astype(BF16)
            xs_ref[:, lo:lo + B_HEAD_DIM] = t[:, B_HEAD_DIM:].astype(BF16)
        ya_ref[...] = _rms(ya_scr[...], oga_ref[...]).astype(BF16)

    @pl.when(i == 0)
    def _():
        z1_scr[...] = jnp.zeros(z1_scr.shape, F32)

    @pl.when(i % 2 == 0)
    def _():
        step(z0_scr, z1_scr)

    @pl.when(i % 2 == 1)
    def _():
        step(z1_scr, z0_scr)


def _in_call(xs, g1, w_in, layer, vg, ws, bs, oga, cds):
    d = xs[0].shape[1]
    n_in = w_in.shape[2]
    c_width = n_in - 2 * A_WIDTH - B_WIDTH
    tm = TOKEN_TILE
    x_specs, tile_bounds = _row_tile_specs(xs, tm)
    nt = tile_bounds[-1]
    t = nt * tm
    this_tile = lambda w: pl.BlockSpec((tm, w), lambda i: (jnp.minimum(i, nt - 1), 0))
    prev_tile = lambda w: pl.BlockSpec((tm, w), lambda i: (jnp.maximum(i - 1, 0), 0))
    return pl.pallas_call(
        functools.partial(_in_body, tile_bounds=tile_bounds),
        grid=(nt + 1,),
        in_specs=x_specs + [_const_spec((1, d)), _const_spec((d, n_in), layer), _const_spec((1, A_WIDTH)),
                            _const_spec(ws.shape[1:], layer), _const_spec(bs.shape[1:], layer),
                            _const_spec((1, A_WIDTH)),
                            _const_spec(cds.shape)],
        out_specs=[prev_tile(A_WIDTH), this_tile(B_WIDTH), this_tile(B_WIDTH), this_tile(c_width)],
        out_shape=[jax.ShapeDtypeStruct((t, A_WIDTH), BF16), jax.ShapeDtypeStruct((t, B_WIDTH), BF16),
                   jax.ShapeDtypeStruct((t, B_WIDTH), BF16), jax.ShapeDtypeStruct((t, c_width), F32)],
        scratch_shapes=[pltpu.VMEM((tm, 2 * A_WIDTH), F32), pltpu.VMEM((tm, 2 * A_WIDTH), F32),
                        pltpu.VMEM((tm, A_WIDTH), F32)],
        compiler_params=_params("arbitrary"),
        name="in_proj_gmlp",
    )(*xs, g1, w_in, vg, ws, bs, oga, cds)


DFT_HALO = 16


def _dft_body(c_ref, sn_ref, xc_ref, xs_ref, og_ref, rev_ref, o_ref, *, scale):
    tk = o_ref.shape[1]
    a = _dot(c_ref[0], xc_ref[...])
    bn = _dot(sn_ref[0], xs_ref[...])
    o_ref[0] = _rms((a + bn)[:tk] * scale, og_ref[...]).astype(BF16)
    mirrored = _rms((a - bn) * scale, og_ref[...]).astype(BF16)
    o_ref[1] = _dot(rev_ref[...], mirrored).astype(BF16)


def _dft_call(c_half, sn_half, rev, xc, xs, ogb, seq):
    t, w = xc.shape
    nb = t // seq
    nk, rows, _ = c_half.shape
    tk = rows - DFT_HALO
    assert 2 * nk * tk == seq
    scale = 1.0 / math.sqrt(seq * B_HEAD_DIM)
    lhs = pl.BlockSpec((1, rows, seq), lambda k, b: (k, 0, 0))
    rhs = pl.BlockSpec((seq, w), lambda k, b: (b, 0))
    return pl.pallas_call(
        functools.partial(_dft_body, scale=scale),
        grid=(nk, nb),
        in_specs=[lhs, lhs, rhs, rhs, pl.BlockSpec((1, w), lambda k, b: (0, 0)), _const_spec(rev.shape)],
        out_specs=pl.BlockSpec((None, 2, None, tk, w), lambda k, b: (b, 0, k, 0, 0)),
        out_shape=jax.ShapeDtypeStruct((nb, 2, nk, tk, w), BF16),
        compiler_params=_params("parallel", "parallel"),
        name="fnet_seq_dft",
    )(c_half, sn_half, xc, xs, ogb, rev)


def _dft_half_matrices(n, tk):
    nk = n // (2 * tk)
    rows = tk + DFT_HALO
    r = 64
    n_a = -(-((nk - 1) * tk + rows) // r)
    s = jnp.arange(n, dtype=jnp.int32)[None, :]
    unit = 2.0 * math.pi / n
    alpha = (((r * jnp.arange(n_a, dtype=jnp.int32)[:, None]) * s) % n).astype(F32) * unit
    beta = ((jnp.arange(r, dtype=jnp.int32)[:, None] * s) % n).astype(F32) * unit
    ca, sa, cb, sb = jnp.cos(alpha), jnp.sin(alpha), jnp.cos(beta), jnp.sin(beta)
    cos = (ca[:, None, :] * cb[None, :, :] - sa[:, None, :] * sb[None, :, :]).reshape(n_a * r, n).astype(BF16)
    nsin = (-(sa[:, None, :] * cb[None, :, :] + ca[:, None, :] * sb[None, :, :])).reshape(n_a * r, n).astype(BF16)
    tiles = lambda m: jnp.stack([m[j * tk:j * tk + rows] for j in range(nk)], axis=0)
    rev = (jnp.arange(rows)[None, :] == tk - jnp.arange(tk)[:, None]).astype(BF16)
    return tiles(cos), tiles(nsin), rev


def _s5_weights(lam_re, lam_im, log_step, b_re, b_im, c_re, c_im, d_skip):
    L = S5_CHUNK
    G, P = lam_re.shape[1:]
    step = jnp.exp(log_step)[..., None]
    lr, li = lam_re, lam_im
    mag = jnp.exp(lr * step)
    ab_re, ab_im = mag * jnp.cos(li * step), mag * jnp.sin(li * step)
    den = lr * lr + li * li
    nr = ab_re - 1.0
    q_re = (nr * lr + ab_im * li) / den
    q_im = (ab_im * lr - nr * li) / den
    bb_re = q_re[..., None] * b_re[None] - q_im[..., None] * b_im[None]
    bb_im = q_re[..., None] * b_im[None] + q_im[..., None] * b_re[None]
    k = jnp.arange(L + 1, dtype=F32)[:, None, None, None]
    pmag = jnp.exp(k * (lr * step))
    pw_re, pw_im = pmag * jnp.cos(k * (li * step)), pmag * jnp.sin(k * (li * step))
    w_re = pw_re[..., None] * bb_re - pw_im[..., None] * bb_im
    w_im = pw_re[..., None] * bb_im + pw_im[..., None] * bb_re
    kern = (jnp.einsum('gop,ktgpc->ktgoc', c_re, w_re[:L]) - jnp.einsum('gop,ktgpc->ktgoc', c_im, w_im[:L]))
    centre = kern[0, 0] + kern[0, 1] + jnp.eye(C_GROUP) * d_skip.reshape(G, 1, C_GROUP)
    lags = jnp.concatenate([kern[1:, 1][::-1], centre[None], kern[1:, 0]], axis=0)
    lags = lags.transpose(1, 3, 0, 2).reshape(G, C_GROUP, (2 * L - 1) * C_GROUP)
    lags = jnp.pad(lags, ((0, 0), (0, 0), (0, C_GROUP)))
    to_rows = lambda a: a.transpose(1, 0, 3, 2).reshape(G, S5_COLS, P)
    p_mat = jnp.concatenate([to_rows(w_re[:L, 0][::-1]), to_rows(w_re[:L, 1]),
                             to_rows(w_im[:L, 0][::-1]), to_rows(w_im[:L, 1])], axis=-1)
    def readout(pr, pi):
        wr = c_re[None] * pr[:, :, None, :] - c_im[None] * pi[:, :, None, :]
        wi = c_re[None] * pi[:, :, None, :] + c_im[None] * pr[:, :, None, :]
        fl = lambda a: a.transpose(1, 3, 0, 2).reshape(G, P, S5_COLS)
        return fl(wr), fl(-wi)
    qf_re, qf_im = readout(pw_re[1:, 0], pw_im[1:, 0])
    qb_re, qb_im = readout(pw_re[1:, 1][::-1], pw_im[1:, 1][::-1])
    z = jnp.zeros_like(qf_re)
    qf = jnp.concatenate([qf_re, z, qf_im, z], axis=1)
    qb = jnp.concatenate([z, qb_re, z, qb_im], axis=1)
    q_mat = jnp.concatenate([qf, qb], axis=1)
    a_re = jnp.concatenate([pw_re[L, 0], pw_re[L, 1]], axis=-1)
    a_im = jnp.concatenate([pw_im[L, 0], pw_im[L, 1]], axis=-1)
    return p_mat.astype(BF16), lags, q_mat.astype(BF16), a_re, a_im


GROUPS_PER_BLOCK = LANES // C_GROUP


def _s5_body(z_ref, p_ref, lag_ref, q_ref, are_ref, aim_ref, o_ref,
             m_scr, v_scr, u_scr, s_scr, ha_scr, hb_scr, y_scr):
    nchunks = z_ref.shape[1] // S5_CHUNK
    rows = GROUPS_PER_BLOCK
    n = nchunks * rows

    @pl.when(pl.program_id(1) == 0)
    def _():
        width = lag_ref.shape[2]
        for g in range(rows):
            lag = lag_ref[g]
            for i in range(S5_CHUNK):
                start = (S5_CHUNK - 1 - i) * C_GROUP
                window = lag if start == 0 else pltpu.roll(lag, width - start, axis=1)
                m_scr[g, i * C_GROUP:(i + 1) * C_GROUP, :] = window[:, :S5_COLS].astype(BF16)

    for i in range(S5_CHUNK):
        t = z_ref[0, pl.ds(i, nchunks, stride=S5_CHUNK), :].T
        for g in range(rows):
            v_scr[g, i * C_GROUP:(i + 1) * C_GROUP, :] = t[g * C_GROUP:(g + 1) * C_GROUP, :]
    for g in range(rows):
        u = v_scr[g].T.astype(BF16)
        u_scr[g] = u
        s = _dot(u, p_ref[g])
        for part in range(2):
            s_scr[part, pl.ds(g, nchunks, stride=rows), :] = s[:, part * LANES:(part + 1) * LANES]

    ar = are_ref[...]
    ai = aim_ref[...]
    is_fwd = lax.broadcasted_iota(jnp.int32, (rows, LANES), 1) < C_STATE
    zero = jnp.zeros((rows, LANES), F32)
    for part in range(2):
        ha_scr[part, 0:rows, :] = zero
        hb_scr[part, n - rows:n, :] = zero

    def step(c, carry):
        hre, him = carry
        r_f = pl.multiple_of(c * rows, rows)
        r_b = pl.multiple_of((nchunks - 1 - c) * rows, rows)
        sre = jnp.where(is_fwd, s_scr[0, pl.ds(r_f, rows), :], s_scr[0, pl.ds(r_b, rows), :])
        sim = jnp.where(is_fwd, s_scr[1, pl.ds(r_f, rows), :], s_scr[1, pl.ds(r_b, rows), :])
        nre = hre * ar - him * ai + sre
        nim = hre * ai + him * ar + sim
        for part, h in enumerate((nre, nim)):
            ha_scr[part, pl.ds(pl.multiple_of(r_f + rows, rows), rows), :] = h
            hb_scr[part, pl.ds(pl.multiple_of(r_b - rows, rows), rows), :] = h
        return nre, nim

    lax.fori_loop(0, nchunks - 1, step, (zero, zero), unroll=5)

    def group_rows(scr, g):
        return jnp.concatenate([scr[part, pl.ds(g, nchunks, stride=rows), :] for part in range(2)],
                               axis=1).astype(BF16)

    for g in range(rows):
        ha = group_rows(ha_scr, g)
        hb = group_rows(hb_scr, g)
        y = (_dot(u_scr[g], m_scr[g]) + _dot(ha, q_ref[g, 0:S5_COLS, :])
             + _dot(hb, q_ref[g, S5_COLS:2 * S5_COLS, :]))
        t = y.T
        for j in range(S5_CHUNK):
            y_scr[j, g * C_GROUP:(g + 1) * C_GROUP, :] = t[j * C_GROUP:(j + 1) * C_GROUP, :]
    for j in range(S5_CHUNK):
        o_ref[0, pl.ds(j, nchunks, stride=S5_CHUNK), :] = y_scr[j].T


def _s5_mixer(zc, nb, seq, weights, layer):
    p_mat, lags, q_mat, a_re, a_im = weights
    width = zc.shape[1]
    nchunks = seq // S5_CHUNK
    gb = GROUPS_PER_BLOCK
    per_block = lambda a: pl.BlockSpec((None, gb) + a.shape[2:], lambda b, n: (layer, b) + (0,) * (a.ndim - 2))
    seq_block = pl.BlockSpec((1, seq, LANES), lambda b, n: (n, 0, b))
    return pl.pallas_call(
        _s5_body,
        grid=(width // LANES, nb),
        in_specs=[seq_block, per_block(p_mat), per_block(lags), per_block(q_mat), per_block(a_re),
                  per_block(a_im)],
        out_specs=seq_block,
        out_shape=jax.ShapeDtypeStruct((nb, seq, width), F32),
        scratch_shapes=[pltpu.VMEM((gb, S5_COLS, S5_COLS), BF16),
                        pltpu.VMEM((gb, S5_COLS, nchunks), F32), pltpu.VMEM((gb, nchunks, S5_COLS), BF16),
                        pltpu.VMEM((2, nchunks * gb, LANES), F32), pltpu.VMEM((2, nchunks * gb, LANES), F32),
                        pltpu.VMEM((2, nchunks * gb, LANES), F32), pltpu.VMEM((S5_CHUNK, LANES, nchunks), F32)],
        compiler_params=_params("parallel", "arbitrary"),
        name="s5_mixer",
    )(zc.reshape(nb, seq, width), p_mat, lags, q_mat, a_re, a_im).reshape(nb * seq, width)


def _out_body(*refs, tile_bounds):
    nx = len(tile_bounds)
    x_refs = refs[:nx]
    ya_ref, yb_ref, yc_ref, wg_ref, bg_ref, ogc_ref, wo_ref, o_ref = refs[nx:]
    acc = _select_rows(x_refs, tile_bounds) + _dot(ya_ref[...], wo_ref[0:A_WIDTH, :])
    acc = acc + _dot(yb_ref[...], wo_ref[A_WIDTH:A_WIDTH + B_WIDTH, :])
    y = _gelu(yc_ref[...].astype(F32))
    yc = y * _sigmoid(_dot(y.astype(BF16), wg_ref[...]) + bg_ref[...])
    yc = _rms(yc, ogc_ref[...]).astype(BF16)
    o_ref[...] = acc + _dot(yc, wo_ref[A_WIDTH + B_WIDTH:, :])


def _out_call(xs, ya, yb, yc, w_glu, b_glu, ogc, w_out, layer):
    t, cw = yc.shape
    d = w_out.shape[2]
    tm = TOKEN_TILE
    x_specs, tile_bounds = _row_tile_specs(xs, tm)
    assert tile_bounds[-1] * tm == t
    row = lambda w: pl.BlockSpec((tm, w), lambda i: (i, 0))
    _, _, nk, tk, bw = yb.shape
    assert tk == tm

    def yb_index(i):
        j = i % (2 * nk)
        return (i // (2 * nk), j // nk, jnp.where(j < nk, j, 2 * nk - 1 - j), 0, 0)

    return pl.pallas_call(
        functools.partial(_out_body, tile_bounds=tile_bounds),
        grid=(t // tm,),
        in_specs=x_specs + [row(A_WIDTH), pl.BlockSpec((None, None, None, tm, bw), yb_index), row(cw),
                            _const_spec((cw, cw), layer),
                            _const_spec((1, cw)), _const_spec((1, cw)), _const_spec(w_out.shape[1:], layer)],
        out_specs=row(d),
        out_shape=jax.ShapeDtypeStruct((t, d), F32),
        compiler_params=_params("parallel"),
        name="glu_out_proj",
    )(*xs, ya, yb, yc, w_glu, b_glu, ogc, w_out)


def _ffn_body(x_ref, g2_ref, wg_ref, wu_ref, wd_ref, gf_ref, o_ref, h_scr, *, final_norm):
    f = pl.program_id(1)

    @pl.when(f == 0)
    def _():
        x = x_ref[...]
        h_scr[...] = _rms(x, g2_ref[...]).astype(BF16)
        o_ref[...] = x

    h = h_scr[...]
    gate = _dot(h, wg_ref[...])
    a = (gate * _sigmoid(gate) * _dot(h, wu_ref[...])).astype(BF16)
    o_ref[...] += _dot(a, wd_ref[...])

    if final_norm:
        @pl.when(f == pl.num_programs(1) - 1)
        def _():
            o_ref[...] = _rms(o_ref[...], gf_ref[...])


def _ffn_call(x, g2, w_gate, w_up, w_down, layer, gf, final_norm, row_start=0, n_rows=None):
    d = x.shape[1]
    ff = w_gate.shape[2]
    n_rows = x.shape[0] if n_rows is None else n_rows
    tm = min(FFN_TOKEN_TILE, n_rows)
    tf = FF_TILE
    assert n_rows % tm == 0 and row_start % tm == 0 and ff % tf == 0
    first = row_start // tm
    return pl.pallas_call(
        functools.partial(_ffn_body, final_norm=final_norm),
        grid=(n_rows // tm, ff // tf),
        in_specs=[pl.BlockSpec((tm, d), lambda i, f: (i + first, 0)), _const_spec((1, d)),
                  pl.BlockSpec((None, d, tf), lambda i, f: (layer, 0, f)),
                  pl.BlockSpec((None, d, tf), lambda i, f: (layer, 0, f)),
                  pl.BlockSpec((None, tf, d), lambda i, f: (layer, f, 0)), _const_spec((1, d))],
        out_specs=pl.BlockSpec((tm, d), lambda i, f: (i, 0)),
        out_shape=jax.ShapeDtypeStruct((n_rows, d), F32),
        scratch_shapes=[pltpu.VMEM((tm, d), BF16)],
        compiler_params=_params("parallel", "arbitrary"),
        name="swiglu_ffn",
    )(x, g2, w_gate, w_up, w_down, gf)


def _dft_matrices(n):
    s = jnp.arange(n, dtype=jnp.int32)[None, :]
    unit = 2.0 * math.pi / n
    r = math.isqrt(n)
    if r * r != n:
        ang = ((jnp.arange(n, dtype=jnp.int32)[:, None] * s) % n).astype(F32) * unit
        return jnp.cos(ang), -jnp.sin(ang)
    ab = jnp.arange(r, dtype=jnp.int32)[:, None]
    alpha = (((r * ab) * s) % n).astype(F32) * unit
    beta = ((ab * s) % n).astype(F32) * unit
    ca, sa, cb, sb = jnp.cos(alpha), jnp.sin(alpha), jnp.cos(beta), jnp.sin(beta)
    cos = ca[:, None, :] * cb[None, :, :] - sa[:, None, :] * sb[None, :, :]
    sin = sa[:, None, :] * cb[None, :, :] + ca[:, None, :] * sb[None, :, :]
    return cos.reshape(n, n), -sin.reshape(n, n)


def kernel(x_prompt, x_sample, norm1_g, w_in, a_v_g, a_ws, a_bs, c_lam_re, c_lam_im, c_log_step, c_b_re, c_b_im,
           c_c_re, c_c_im, c_d, c_w_glu, c_b_glu, out_norm_g, w_out, norm2_g, w_gate, w_up, w_down, final_g):
    seq, d = x_prompt.shape[1:]
    assert x_sample.shape[1:] == (seq, d)
    nb = x_prompt.shape[0] + x_sample.shape[0]
    n_prompt = x_prompt.shape[0] * seq
    depth = w_in.shape[0]
    xs = (x_prompt.reshape(-1, d), x_sample.reshape(-1, d))

    c_half, sn_half, rev = _dft_half_matrices(seq, min(DFT_ROW_TILE, seq // 2))
    cd, sdn = _dft_matrices(B_HEAD_DIM)
    cds = jnp.concatenate([cd, -sdn], axis=1).astype(BF16)
    row = lambda v: v.reshape(1, -1)
    w_in, w_out, c_w_glu, w_gate, w_up, w_down = (
        w.astype(BF16) for w in (w_in, w_out, c_w_glu, w_gate, w_up, w_down))
    s5w = jax.vmap(_s5_weights)(c_lam_re, c_lam_im, c_log_step, c_b_re, c_b_im, c_c_re, c_c_im, c_d)

    ws = a_ws.astype(BF16)
    bs = jnp.broadcast_to(a_bs[..., None], a_bs.shape + (A_HEAD_DIM,))

    for l in range(depth):
        og = out_norm_g[l]
        ya, xc, xsin, zc = _in_call(xs, row(norm1_g[l]), w_in, l, row(a_v_g[l]), ws, bs,
                                    row(og[:A_WIDTH]), cds)
        yb = _dft_call(c_half, sn_half, rev, xc, xsin, row(og[A_WIDTH:A_WIDTH + B_WIDTH]), seq)
        yc = _s5_mixer(zc, nb, seq, s5w, l)
        x = _out_call(xs, ya, yb, yc, c_w_glu, row(c_b_glu[l]), row(og[A_WIDTH + B_WIDTH:]), w_out, l)
        ffn = functools.partial(_ffn_call, x, row(norm2_g[l]), w_gate, w_up, w_down, l, row(final_g))
        if l < depth - 1:
            xs = (ffn(final_norm=False),)
        else:
            y_prompt = ffn(final_norm=True, row_start=0, n_rows=n_prompt)
            y_sample = ffn(final_norm=True, row_start=n_prompt, n_rows=x.shape[0] - n_prompt)

    return (y_prompt.reshape(x_prompt.shape), y_sample.reshape(x_sample.shape))
```

```python
import functools
import math

import jax
import jax.numpy as jnp
from jax import lax
from jax.experimental import pallas as pl
from jax.experimental.pallas import tpu as pltpu

F32 = jnp.float32
BF16 = jnp.bfloat16

EPS = 1e-6
A_HEADS = 6
A_HEAD_DIM = 128
A_WIDTH = A_HEADS * A_HEAD_DIM
CHUNK = 128
B_HEADS = 4
B_HEAD_DIM = 128
B_WIDTH = B_HEADS * B_HEAD_DIM
C_GROUP = 16
C_STATE = 64

LANES = 128
SUBLANES = 8
S5_CHUNK = 16
S5_COLS = S5_CHUNK * C_GROUP
VMEM_LIMIT_BYTES = 56 * 1024 * 1024

TOKEN_TILE = 512
FFN_TOKEN_TILE = 1024
FF_TILE = 512
DFT_ROW_TILE = 512
GELU_C = math.sqrt(2.0 / math.pi)


def _gelu(x):
    return 0.5 * x * (1.0 + jnp.tanh(GELU_C * (x + 0.044715 * (x * x * x))))


def _sigmoid(x):
    return 1.0 / (1.0 + jnp.exp(-x))


def _rms(x, g):
    return x * lax.rsqrt(jnp.mean(x * x, axis=-1, keepdims=True) + EPS) * g


def _dot(a, b):
    return jnp.dot(a, b, preferred_element_type=F32)


def _const_spec(shape, layer=None):
    nd = len(shape)
    if layer is None:
        return pl.BlockSpec(shape, lambda *_: (0,) * nd, pipeline_mode=pl.Buffered(1))
    return pl.BlockSpec((None,) + tuple(shape), lambda *_: (layer,) + (0,) * nd, pipeline_mode=pl.Buffered(1))


def _row_tile_specs(xs, tm):
    specs, bounds, start = [], [], 0
    for x in xs:
        n = x.shape[0] // tm
        assert n * tm == x.shape[0] and tm % CHUNK == 0, (x.shape, tm)
        specs.append(pl.BlockSpec((tm, x.shape[1]),
                                  lambda i, *_, start=start, n=n: (jnp.clip(i - start, 0, n - 1), 0)))
        start += n
        bounds.append(start)
    return specs, tuple(bounds)


def _select_rows(x_refs, bounds):
    x = x_refs[-1][...]
    for k in range(len(x_refs) - 2, -1, -1):
        x = jnp.where(pl.program_id(0) < bounds[k], x_refs[k][...], x)
    return x


def _params(*sem):
    return pltpu.CompilerParams(dimension_semantics=sem, vmem_limit_bytes=VMEM_LIMIT_BYTES)


def _in_body(*refs, tile_bounds):
    nx = len(tile_bounds)
    x_refs = refs[:nx]
    (w_ref, vg_ref, ws_ref, bs_ref, oga_ref, cds_ref,
     ya_ref, xc_ref, xs_ref, zc_ref, z0_scr, z1_scr, ya_scr) = refs[nx:]
    tm = ya_ref.shape[0]
    nch = tm // CHUNK
    i = pl.program_id(0)

    def step(z_new, z_old):
        x = _select_rows(x_refs, tile_bounds)
        inv = lax.rsqrt(jnp.mean(x * x, axis=-1, keepdims=True) + EPS)
        h = x.astype(BF16)
        zb = (_dot(h, w_ref[:, 2 * A_WIDTH:2 * A_WIDTH + B_WIDTH]) * inv).astype(BF16)
        slab = 2 * A_WIDTH // A_HEADS
        for hd in range(A_HEADS):
            z_new[:, hd * slab:(hd + 1) * slab] = _dot(h, w_ref[:, hd * slab:(hd + 1) * slab]) * inv
            lo = hd * A_HEAD_DIM
            u = _gelu(z_old[:, lo:lo + A_HEAD_DIM])
            v = _gelu(z_old[:, A_WIDTH + lo:A_WIDTH + lo + A_HEAD_DIM])
            v = _rms(v, vg_ref[:, lo:lo + A_HEAD_DIM]).astype(BF16)
            v_wide = jnp.concatenate([v[c * CHUNK:(c + 1) * CHUNK, :] for c in range(nch)], axis=1)
            mixed = _dot(ws_ref[hd], v_wide)
            for c in range(nch):
                r = c * CHUNK
                ya_scr[r:r + CHUNK, lo:lo + A_HEAD_DIM] = u[r:r + CHUNK, :] * (
                    mixed[:, c * A_HEAD_DIM:(c + 1) * A_HEAD_DIM] + bs_ref[hd])
        zc_ref[...] = _dot(h, w_ref[:, 2 * A_WIDTH + B_WIDTH:]) * inv
        for g in range(B_HEADS):
            lo = g * B_HEAD_DIM
            t = _dot(zb[:, lo:lo + B_HEAD_DIM], cds_ref[...])
            xc_ref[:, lo:lo + B_HEAD_DIM] = t[:, :B_HEAD_DIM].astype(BF16)
            xs_ref[:, lo:lo + B_HEAD_DIM] = t[:, B_HEAD_DIM:].astype(BF16)
        ya_ref[...] = _rms(ya_scr[...], oga_ref[...]).astype(BF16)

    @pl.when(i == 0)
    def _():
        z1_scr[...] = jnp.zeros(z1_scr.shape, F32)

    @pl.when(i % 2 == 0)
    def _():
        step(z0_scr, z1_scr)

    @pl.when(i % 2 == 1)
    def _():
        step(z1_scr, z0_scr)


def _in_call(xs, w_in, layer, vg, ws, bs, oga, cds):
    d = xs[0].shape[1]
    n_in = w_in.shape[2]
    c_width = n_in - 2 * A_WIDTH - B_WIDTH
    tm = TOKEN_TILE
    x_specs, tile_bounds = _row_tile_specs(xs, tm)
    nt = tile_bounds[-1]
    t = nt * tm
    this_tile = lambda w: pl.BlockSpec((tm, w), lambda i: (jnp.minimum(i, nt - 1), 0))
    prev_tile = lambda w: pl.BlockSpec((tm, w), lambda i: (jnp.maximum(i - 1, 0), 0))
    return pl.pallas_call(
        functools.partial(_in_body, tile_bounds=tile_bounds),
        grid=(nt + 1,),
        in_specs=x_specs + [_const_spec((d, n_in), layer), _const_spec((1, A_WIDTH)),
                            _const_spec(ws.shape[1:], layer), _const_spec(bs.shape[1:], layer),
                            _const_spec((1, A_WIDTH)),
                            _const_spec(cds.shape)],
        out_specs=[prev_tile(A_WIDTH), this_tile(B_WIDTH), this_tile(B_WIDTH), this_tile(c_width)],
        out_shape=[jax.ShapeDtypeStruct((t, A_WIDTH), BF16), jax.ShapeDtypeStruct((t, B_WIDTH), BF16),
                   jax.ShapeDtypeStruct((t, B_WIDTH), BF16), jax.ShapeDtypeStruct((t, c_width), F32)],
        scratch_shapes=[pltpu.VMEM((tm, 2 * A_WIDTH), F32), pltpu.VMEM((tm, 2 * A_WIDTH), F32),
                        pltpu.VMEM((tm, A_WIDTH), F32)],
        compiler_params=_params("arbitrary"),
        name="in_proj_gmlp",
    )(*xs, w_in, vg, ws, bs, oga, cds)


DFT_HALO = 16


def _dft_body(c_ref, sn_ref, xc_ref, xs_ref, og_ref, rev_ref, o_ref, *, scale):
    tk = o_ref.shape[1]
    a = _dot(c_ref[...], xc_ref[...])
    bn = _dot(sn_ref[...], xs_ref[...])
    o_ref[0] = _rms((a + bn)[:tk] * scale, og_ref[...]).astype(BF16)
    mirrored = _rms((a - bn) * scale, og_ref[...]).astype(BF16)
    o_ref[1] = _dot(rev_ref[...], mirrored).astype(BF16)


def _dft_call(c_half, sn_half, rev, xc, xs, ogb, seq):
    t, w = xc.shape
    nb = t // seq
    tk, rows = rev.shape
    nk = seq // (2 * tk)
    assert 2 * nk * tk == seq and c_half.shape[0] >= (nk - 1) * tk + rows
    scale = 1.0 / math.sqrt(seq * B_HEAD_DIM)
    lhs = pl.BlockSpec((pl.Element(rows), pl.Element(seq)), lambda k, b: (k * tk, 0))
    rhs = pl.BlockSpec((seq, w), lambda k, b: (b, 0))
    return pl.pallas_call(
        functools.partial(_dft_body, scale=scale),
        grid=(nk, nb),
        in_specs=[lhs, lhs, rhs, rhs, pl.BlockSpec((1, w), lambda k, b: (0, 0)), _const_spec(rev.shape)],
        out_specs=pl.BlockSpec((None, 2, None, tk, w), lambda k, b: (b, 0, k, 0, 0)),
        out_shape=jax.ShapeDtypeStruct((nb, 2, nk, tk, w), BF16),
        compiler_params=_params("parallel", "parallel"),
        name="fnet_seq_dft",
    )(c_half, sn_half, xc, xs, ogb, rev)


def _dft_half_matrices(n, tk):
    nk = n // (2 * tk)
    rows = tk + DFT_HALO
    r = 64
    n_a = -(-((nk - 1) * tk + rows) // r)
    s = jnp.arange(n, dtype=jnp.int32)[None, :]
    unit = 2.0 * math.pi / n
    alpha = (((r * jnp.arange(n_a, dtype=jnp.int32)[:, None]) * s) % n).astype(F32) * unit
    beta = ((jnp.arange(r, dtype=jnp.int32)[:, None] * s) % n).astype(F32) * unit
    ca, sa, cb, sb = jnp.cos(alpha), jnp.sin(alpha), jnp.cos(beta), jnp.sin(beta)
    cos = (ca[:, None, :] * cb[None, :, :] - sa[:, None, :] * sb[None, :, :]).reshape(n_a * r, n).astype(BF16)
    nsin = (-(sa[:, None, :] * cb[None, :, :] + ca[:, None, :] * sb[None, :, :])).reshape(n_a * r, n).astype(BF16)
    rev = (jnp.arange(rows)[None, :] == tk - jnp.arange(tk)[:, None]).astype(BF16)
    return cos, nsin, rev


def _s5_weights(lam_re, lam_im, log_step, b_re, b_im, c_re, c_im, d_skip):
    L = S5_CHUNK
    G, P = lam_re.shape[1:]
    step = jnp.exp(log_step)[..., None]
    lr, li = lam_re, lam_im
    mag = jnp.exp(lr * step)
    ab_re, ab_im = mag * jnp.cos(li * step), mag * jnp.sin(li * step)
    den = lr * lr + li * li
    nr = ab_re - 1.0
    q_re = (nr * lr + ab_im * li) / den
    q_im = (ab_im * lr - nr * li) / den
    bb_re = q_re[..., None] * b_re[None] - q_im[..., None] * b_im[None]
    bb_im = q_re[..., None] * b_im[None] + q_im[..., None] * b_re[None]
    k = jnp.arange(L + 1, dtype=F32)[:, None, None, None]
    pmag = jnp.exp(k * (lr * step))
    pw_re, pw_im = pmag * jnp.cos(k * (li * step)), pmag * jnp.sin(k * (li * step))
    w_re = pw_re[..., None] * bb_re - pw_im[..., None] * bb_im
    w_im = pw_re[..., None] * bb_im + pw_im[..., None] * bb_re
    kern = (jnp.einsum('gop,ktgpc->ktgoc', c_re, w_re[:L]) - jnp.einsum('gop,ktgpc->ktgoc', c_im, w_im[:L]))
    centre = kern[0, 0] + kern[0, 1] + jnp.eye(C_GROUP) * d_skip.reshape(G, 1, C_GROUP)
    lags = jnp.concatenate([kern[1:, 1][::-1], centre[None], kern[1:, 0]], axis=0)
    lags = lags.transpose(1, 3, 0, 2).reshape(G, C_GROUP, (2 * L - 1) * C_GROUP)
    lags = jnp.pad(lags, ((0, 0), (0, 0), (0, C_GROUP)))
    to_rows = lambda a: a.transpose(1, 0, 3, 2).reshape(G, S5_COLS, P)
    p_mat = jnp.concatenate([to_rows(w_re[:L, 0][::-1]), to_rows(w_re[:L, 1]),
                             to_rows(w_im[:L, 0][::-1]), to_rows(w_im[:L, 1])], axis=-1)
    def readout(pr, pi):
        wr = c_re[None] * pr[:, :, None, :] - c_im[None] * pi[:, :, None, :]
        wi = c_re[None] * pi[:, :, None, :] + c_im[None] * pr[:, :, None, :]
        fl = lambda a: a.transpose(1, 3, 0, 2).reshape(G, P, S5_COLS)
        return fl(wr), fl(-wi)
    qf_re, qf_im = readout(pw_re[1:, 0], pw_im[1:, 0])
    qb_re, qb_im = readout(pw_re[1:, 1][::-1], pw_im[1:, 1][::-1])
    z = jnp.zeros_like(qf_re)
    qf = jnp.concatenate([qf_re, z, qf_im, z], axis=1)
    qb = jnp.concatenate([z, qb_re, z, qb_im], axis=1)
    q_mat = jnp.concatenate([qf, qb], axis=1)
    a_re = jnp.concatenate([pw_re[L, 0], pw_re[L, 1]], axis=-1)
    a_im = jnp.concatenate([pw_im[L, 0], pw_im[L, 1]], axis=-1)
    return p_mat.astype(BF16), lags, q_mat.astype(BF16), a_re, a_im


GROUPS_PER_BLOCK = LANES // C_GROUP


def _s5_body(z_ref, p_ref, lag_ref, q_ref, are_ref, aim_ref, o_ref,
             m_scr, v_scr, u_scr, s_scr, ha_scr, hb_scr, y_scr):
    nchunks = z_ref.shape[1] // S5_CHUNK
    rows = GROUPS_PER_BLOCK
    n = nchunks * rows

    @pl.when(pl.program_id(1) == 0)
    def _():
        width = lag_ref.shape[2]
        for g in range(rows):
            lag = lag_ref[g]
            for i in range(S5_CHUNK):
                start = (S5_CHUNK - 1 - i) * C_GROUP
                window = lag if start == 0 else pltpu.roll(lag, width - start, axis=1)
                m_scr[g, i * C_GROUP:(i + 1) * C_GROUP, :] = window[:, :S5_COLS].astype(BF16)

    for i in range(S5_CHUNK):
        t = z_ref[0, pl.ds(i, nchunks, stride=S5_CHUNK), :].T
        for g in range(rows):
            v_scr[g, i * C_GROUP:(i + 1) * C_GROUP, :] = t[g * C_GROUP:(g + 1) * C_GROUP, :]
    for g in range(rows):
        u = v_scr[g].T.astype(BF16)
        u_scr[g] = u
        s = _dot(u, p_ref[g])
        for part in range(2):
            s_scr[part, pl.ds(g, nchunks, stride=rows), :] = s[:, part * LANES:(part + 1) * LANES]

    ar = are_ref[...]
    ai = aim_ref[...]
    is_fwd = lax.broadcasted_iota(jnp.int32, (rows, LANES), 1) < C_STATE
    zero = jnp.zeros((rows, LANES), F32)
    for part in range(2):
        ha_scr[part, 0:rows, :] = zero
        hb_scr[part, n - rows:n, :] = zero

    def step(c, carry):
        hre, him = carry
        r_f = pl.multiple_of(c * rows, rows)
        r_b = pl.multiple_of((nchunks - 1 - c) * rows, rows)
        sre = jnp.where(is_fwd, s_scr[0, pl.ds(r_f, rows), :], s_scr[0, pl.ds(r_b, rows), :])
        sim = jnp.where(is_fwd, s_scr[1, pl.ds(r_f, rows), :], s_scr[1, pl.ds(r_b, rows), :])
        nre = hre * ar - him * ai + sre
        nim = hre * ai + him * ar + sim
        for part, h in enumerate((nre, nim)):
            ha_scr[part, pl.ds(pl.multiple_of(r_f + rows, rows), rows), :] = h
            hb_scr[part, pl.ds(pl.multiple_of(r_b - rows, rows), rows), :] = h
        return nre, nim

    lax.fori_loop(0, nchunks - 1, step, (zero, zero), unroll=5)

    def group_rows(scr, g):
        return jnp.concatenate([scr[part, pl.ds(g, nchunks, stride=rows), :] for part in range(2)],
                               axis=1).astype(BF16)

    for g in range(rows):
        ha = group_rows(ha_scr, g)
        hb = group_rows(hb_scr, g)
        y = (_dot(u_scr[g], m_scr[g]) + _dot(ha, q_ref[g, 0:S5_COLS, :])
             + _dot(hb, q_ref[g, S5_COLS:2 * S5_COLS, :]))
        t = y.T
        for j in range(S5_CHUNK):
            y_scr[j, g * C_GROUP:(g + 1) * C_GROUP, :] = t[j * C_GROUP:(j + 1) * C_GROUP, :]
    for j in range(S5_CHUNK):
        o_ref[0, pl.ds(j, nchunks, stride=S5_CHUNK), :] = y_scr[j].T


def _s5_mixer(zc, nb, seq, weights, layer):
    p_mat, lags, q_mat, a_re, a_im = weights
    width = zc.shape[1]
    nchunks = seq // S5_CHUNK
    gb = GROUPS_PER_BLOCK
    per_block = lambda a: pl.BlockSpec((None, gb) + a.shape[2:], lambda b, n: (layer, b) + (0,) * (a.ndim - 2))
    seq_block = pl.BlockSpec((1, seq, LANES), lambda b, n: (n, 0, b))
    return pl.pallas_call(
        _s5_body,
        grid=(width // LANES, nb),
        in_specs=[seq_block, per_block(p_mat), per_block(lags), per_block(q_mat), per_block(a_re),
                  per_block(a_im)],
        out_specs=seq_block,
        out_shape=jax.ShapeDtypeStruct((nb, seq, width), F32),
        scratch_shapes=[pltpu.VMEM((gb, S5_COLS, S5_COLS), BF16),
                        pltpu.VMEM((gb, S5_COLS, nchunks), F32), pltpu.VMEM((gb, nchunks, S5_COLS), BF16),
                        pltpu.VMEM((2, nchunks * gb, LANES), F32), pltpu.VMEM((2, nchunks * gb, LANES), F32),
                        pltpu.VMEM((2, nchunks * gb, LANES), F32), pltpu.VMEM((S5_CHUNK, LANES, nchunks), F32)],
        compiler_params=_params("parallel", "arbitrary"),
        name="s5_mixer",
    )(zc.reshape(nb, seq, width), p_mat, lags, q_mat, a_re, a_im).reshape(nb * seq, width)


def _out_body(*refs, tile_bounds):
    nx = len(tile_bounds)
    x_refs = refs[:nx]
    ya_ref, yb_ref, yc_ref, wg_ref, bg_ref, ogc_ref, wo_ref, o_ref = refs[nx:]
    acc = _select_rows(x_refs, tile_bounds) + _dot(ya_ref[...], wo_ref[0:A_WIDTH, :])
    acc = acc + _dot(yb_ref[...], wo_ref[A_WIDTH:A_WIDTH + B_WIDTH, :])
    y = _gelu(yc_ref[...].astype(F32))
    yc = y * _sigmoid(_dot(y.astype(BF16), wg_ref[...]) + bg_ref[...])
    yc = _rms(yc, ogc_ref[...]).astype(BF16)
    o_ref[...] = acc + _dot(yc, wo_ref[A_WIDTH + B_WIDTH:, :])


def _out_call(xs, ya, yb, yc, w_glu, b_glu, ogc, w_out, layer):
    t, cw = yc.shape
    d = w_out.shape[2]
    tm = TOKEN_TILE
    x_specs, tile_bounds = _row_tile_specs(xs, tm)
    assert tile_bounds[-1] * tm == t
    row = lambda w: pl.BlockSpec((tm, w), lambda i: (i, 0))
    _, _, nk, tk, bw = yb.shape
    assert tk == tm

    def yb_index(i):
        j = i % (2 * nk)
        return (i // (2 * nk), j // nk, jnp.where(j < nk, j, 2 * nk - 1 - j), 0, 0)

    return pl.pallas_call(
        functools.partial(_out_body, tile_bounds=tile_bounds),
        grid=(t // tm,),
        in_specs=x_specs + [row(A_WIDTH), pl.BlockSpec((None, None, None, tm, bw), yb_index), row(cw),
                            _const_spec((cw, cw), layer),
                            _const_spec((1, cw)), _const_spec((1, cw)), _const_spec(w_out.shape[1:], layer)],
        out_specs=row(d),
        out_shape=jax.ShapeDtypeStruct((t, d), F32),
        compiler_params=_params("parallel"),
        name="glu_out_proj",
    )(*xs, ya, yb, yc, w_glu, b_glu, ogc, w_out)


def _ffn_body(x_ref, wg_ref, wu_ref, wd_ref, gf_ref, o_ref, h_scr, inv_scr, *, final_norm):
    f = pl.program_id(1)

    @pl.when(f == 0)
    def _():
        x = x_ref[...]
        h_scr[...] = x.astype(BF16)
        inv_scr[...] = jnp.broadcast_to(lax.rsqrt(jnp.mean(x * x, axis=-1, keepdims=True) + EPS), inv_scr.shape)
        o_ref[...] = x

    h = h_scr[...]
    inv = inv_scr[:, 0:1]
    gate = _dot(h, wg_ref[...]) * inv
    a = (gate * _sigmoid(gate) * (_dot(h, wu_ref[...]) * inv)).astype(BF16)
    o_ref[...] += _dot(a, wd_ref[...])

    if final_norm:
        @pl.when(f == pl.num_programs(1) - 1)
        def _():
            o_ref[...] = _rms(o_ref[...], gf_ref[...])


def _ffn_call(x, w_gate, w_up, w_down, layer, gf, final_norm, row_start=0, n_rows=None):
    d = x.shape[1]
    ff = w_gate.shape[2]
    n_rows = x.shape[0] if n_rows is None else n_rows
    tm = min(FFN_TOKEN_TILE, n_rows)
    tf = FF_TILE
    assert n_rows % tm == 0 and row_start % tm == 0 and ff % tf == 0
    first = row_start // tm
    return pl.pallas_call(
        functools.partial(_ffn_body, final_norm=final_norm),
        grid=(n_rows // tm, ff // tf),
        in_specs=[pl.BlockSpec((tm, d), lambda i, f: (i + first, 0)),
                  pl.BlockSpec((None, d, tf), lambda i, f: (layer, 0, f)),
                  pl.BlockSpec((None, d, tf), lambda i, f: (layer, 0, f)),
                  pl.BlockSpec((None, tf, d), lambda i, f: (layer, f, 0)), _const_spec((1, d))],
        out_specs=pl.BlockSpec((tm, d), lambda i, f: (i, 0)),
        out_shape=jax.ShapeDtypeStruct((n_rows, d), F32),
        scratch_shapes=[pltpu.VMEM((tm, d), BF16), pltpu.VMEM((tm, LANES), F32)],
        compiler_params=_params("parallel", "arbitrary"),
        name="swiglu_ffn",
    )(x, w_gate, w_up, w_down, gf)


def _dft_matrices(n):
    s = jnp.arange(n, dtype=jnp.int32)[None, :]
    unit = 2.0 * math.pi / n
    r = math.isqrt(n)
    if r * r != n:
        ang = ((jnp.arange(n, dtype=jnp.int32)[:, None] * s) % n).astype(F32) * unit
        return jnp.cos(ang), -jnp.sin(ang)
    ab = jnp.arange(r, dtype=jnp.int32)[:, None]
    alpha = (((r * ab) * s) % n).astype(F32) * unit
    beta = ((ab * s) % n).astype(F32) * unit
    ca, sa, cb, sb = jnp.cos(alpha), jnp.sin(alpha), jnp.cos(beta), jnp.sin(beta)
    cos = ca[:, None, :] * cb[None, :, :] - sa[:, None, :] * sb[None, :, :]
    sin = sa[:, None, :] * cb[None, :, :] + ca[:, None, :] * sb[None, :, :]
    return cos.reshape(n, n), -sin.reshape(n, n)


def kernel(x_prompt, x_sample, norm1_g, w_in, a_v_g, a_ws, a_bs, c_lam_re, c_lam_im, c_log_step, c_b_re, c_b_im,
           c_c_re, c_c_im, c_d, c_w_glu, c_b_glu, out_norm_g, w_out, norm2_g, w_gate, w_up, w_down, final_g):
    seq, d = x_prompt.shape[1:]
    assert x_sample.shape[1:] == (seq, d)
    nb = x_prompt.shape[0] + x_sample.shape[0]
    n_prompt = x_prompt.shape[0] * seq
    depth = w_in.shape[0]
    xs = (x_prompt.reshape(-1, d), x_sample.reshape(-1, d))

    c_half, sn_half, rev = _dft_half_matrices(seq, min(DFT_ROW_TILE, seq // 2))
    cd, sdn = _dft_matrices(B_HEAD_DIM)
    cds = jnp.concatenate([cd, -sdn], axis=1).astype(BF16)
    row = lambda v: v.reshape(1, -1)
    g1, g2 = norm1_g[:, :, None], norm2_g[:, :, None]
    w_in, w_out, c_w_glu, w_gate, w_up, w_down = (
        w.astype(BF16) for w in (w_in * g1, w_out, c_w_glu, w_gate * g2, w_up * g2, w_down))
    s5w = jax.vmap(_s5_weights)(c_lam_re, c_lam_im, c_log_step, c_b_re, c_b_im, c_c_re, c_c_im, c_d)

    ws = a_ws.astype(BF16)
    bs = jnp.broadcast_to(a_bs[..., None], a_bs.shape + (A_HEAD_DIM,))

    for l in range(depth):
        og = out_norm_g[l]
        ya, xc, xsin, zc = _in_call(xs, w_in, l, row(a_v_g[l]), ws, bs, row(og[:A_WIDTH]), cds)
        yb = _dft_call(c_half, sn_half, rev, xc, xsin, row(og[A_WIDTH:A_WIDTH + B_WIDTH]), seq)
        yc = _s5_mixer(zc, nb, seq, s5w, l)
        x = _out_call(xs, ya, yb, yc, c_w_glu, row(c_b_glu[l]), row(og[A_WIDTH + B_WIDTH:]), w_out, l)
        ffn = functools.partial(_ffn_call, x, w_gate, w_up, w_down, l, row(final_g))
        if l < depth - 1:
            xs = (ffn(final_norm=False),)
        else:
            y_prompt = ffn(final_norm=True, row_start=0, n_rows=n_prompt)
            y_sample = ffn(final_norm=True, row_start=n_prompt, n_rows=x.shape[0] - n_prompt)

    return (y_prompt.reshape(x_prompt.shape), y_sample.reshape(x_sample.shape))
```

```python
import functools
import math

import jax
import jax.numpy as jnp
from jax import lax
from jax.experimental import pallas as pl
from jax.experimental.pallas import tpu as pltpu

F32 = jnp.float32
BF16 = jnp.bfloat16

EPS = 1e-6
A_HEADS = 6
A_HEAD_DIM = 128
A_WIDTH = A_HEADS * A_HEAD_DIM
CHUNK = 128
B_HEADS = 4
B_HEAD_DIM = 128
B_WIDTH = B_HEADS * B_HEAD_DIM
C_GROUP = 16
C_STATE = 64

LANES = 128
SUBLANES = 8
S5_CHUNK = 16
S5_COLS = S5_CHUNK * C_GROUP
VMEM_LIMIT_BYTES = 56 * 1024 * 1024

TOKEN_TILE = 512
FFN_TOKEN_TILE = 1024
FF_TILE = 512
DFT_ROW_TILE = 512
GELU_C = math.sqrt(2.0 / math.pi)


def _gelu(x):
    return 0.5 * x * (1.0 + jnp.tanh(GELU_C * (x + 0.044715 * (x * x * x))))


def _sigmoid(x):
    return 1.0 / (1.0 + jnp.exp(-x))


def _rms(x, g):
    return x * lax.rsqrt(jnp.mean(x * x, axis=-1, keepdims=True) + EPS) * g


def _dot(a, b):
    return jnp.dot(a, b, preferred_element_type=F32)


def _const_spec(shape, layer=None):
    nd = len(shape)
    if layer is None:
        return pl.BlockSpec(shape, lambda *_: (0,) * nd, pipeline_mode=pl.Buffered(1))
    return pl.BlockSpec((None,) + tuple(shape), lambda *_: (layer,) + (0,) * nd, pipeline_mode=pl.Buffered(1))


def _row_tile_specs(xs, tm):
    specs, bounds, start = [], [], 0
    for x in xs:
        n = x.shape[0] // tm
        assert n * tm == x.shape[0] and tm % CHUNK == 0, (x.shape, tm)
        specs.append(pl.BlockSpec((tm, x.shape[1]),
                                  lambda i, *_, start=start, n=n: (jnp.clip(i - start, 0, n - 1), 0)))
        start += n
        bounds.append(start)
    return specs, tuple(bounds)


def _select_rows(x_refs, bounds):
    x = x_refs[-1][...]
    for k in range(len(x_refs) - 2, -1, -1):
        x = jnp.where(pl.program_id(0) < bounds[k], x_refs[k][...], x)
    return x


def _params(*sem):
    return pltpu.CompilerParams(dimension_semantics=sem, vmem_limit_bytes=VMEM_LIMIT_BYTES)


def _in_body(*refs, tile_bounds):
    nx = len(tile_bounds)
    x_refs = refs[:nx]
    (w_ref, vg_ref, ws_ref, bs_ref, oga_ref, cds_ref,
     ya_ref, xc_ref, xs_ref, zc_ref, z0_scr, z1_scr, ya_scr) = refs[nx:]
    tm = ya_ref.shape[0]
    nch = tm // CHUNK
    i = pl.program_id(0)

    def step(z_new, z_old):
        x = _select_rows(x_refs, tile_bounds)
        h = (x * lax.rsqrt(jnp.mean(x * x, axis=-1, keepdims=True) + EPS)).astype(BF16)
        zb = _dot(h, w_ref[:, 2 * A_WIDTH:2 * A_WIDTH + B_WIDTH]).astype(BF16)
        slab = 2 * A_WIDTH // A_HEADS
        for hd in range(A_HEADS):
            z_new[:, hd * slab:(hd + 1) * slab] = _dot(h, w_ref[:, hd * slab:(hd + 1) * slab])
            lo = hd * A_HEAD_DIM
            u = _gelu(z_old[:, lo:lo + A_HEAD_DIM])
            v = _gelu(z_old[:, A_WIDTH + lo:A_WIDTH + lo + A_HEAD_DIM])
            v = _rms(v, vg_ref[:, lo:lo + A_HEAD_DIM]).astype(BF16)
            v_wide = jnp.concatenate([v[c * CHUNK:(c + 1) * CHUNK, :] for c in range(nch)], axis=1)
            mixed = _dot(ws_ref[hd], v_wide)
            for c in range(nch):
                r = c * CHUNK
                ya_scr[r:r + CHUNK, lo:lo + A_HEAD_DIM] = u[r:r + CHUNK, :] * (
                    mixed[:, c * A_HEAD_DIM:(c + 1) * A_HEAD_DIM] + bs_ref[hd])
        zc_ref[...] = _dot(h, w_ref[:, 2 * A_WIDTH + B_WIDTH:])
        for g in range(B_HEADS):
            lo = g * B_HEAD_DIM
            t = _dot(zb[:, lo:lo + B_HEAD_DIM], cds_ref[...])
            xc_ref[:, lo:lo + B_HEAD_DIM] = t[:, :B_HEAD_DIM].astype(BF16)
            xs_ref[:, lo:lo + B_HEAD_DIM] = t[:, B_HEAD_DIM:].astype(BF16)
        ya_ref[...] = _rms(ya_scr[...], oga_ref[...]).astype(BF16)

    @pl.when(i == 0)
    def _():
        z1_scr[...] = jnp.zeros(z1_scr.shape, F32)

    @pl.when(i % 2 == 0)
    def _():
        step(z0_scr, z1_scr)

    @pl.when(i % 2 == 1)
    def _():
        step(z1_scr, z0_scr)


def _in_call(xs, w_in, layer, vg, ws, bs, oga, cds):
    d = xs[0].shape[1]
    n_in = w_in.shape[2]
    c_width = n_in - 2 * A_WIDTH - B_WIDTH
    tm = TOKEN_TILE
    x_specs, tile_bounds = _row_tile_specs(xs, tm)
    nt = tile_bounds[-1]
    t = nt * tm
    this_tile = lambda w: pl.BlockSpec((tm, w), lambda i: (jnp.minimum(i, nt - 1), 0))
    prev_tile = lambda w: pl.BlockSpec((tm, w), lambda i: (jnp.maximum(i - 1, 0), 0))
    return pl.pallas_call(
        functools.partial(_in_body, tile_bounds=tile_bounds),
        grid=(nt + 1,),
        in_specs=x_specs + [_const_spec((d, n_in), layer), _const_spec((1, A_WIDTH)),
                            _const_spec(ws.shape[1:], layer), _const_spec(bs.shape[1:], layer),
                            _const_spec((1, A_WIDTH)),
                            _const_spec(cds.shape)],
        out_specs=[prev_tile(A_WIDTH), this_tile(B_WIDTH), this_tile(B_WIDTH), this_tile(c_width)],
        out_shape=[jax.ShapeDtypeStruct((t, A_WIDTH), BF16), jax.ShapeDtypeStruct((t, B_WIDTH), BF16),
                   jax.ShapeDtypeStruct((t, B_WIDTH), BF16), jax.ShapeDtypeStruct((t, c_width), F32)],
        scratch_shapes=[pltpu.VMEM((tm, 2 * A_WIDTH), F32), pltpu.VMEM((tm, 2 * A_WIDTH), F32),
                        pltpu.VMEM((tm, A_WIDTH), F32)],
        compiler_params=_params("arbitrary"),
        name="in_proj_gmlp",
    )(*xs, w_in, vg, ws, bs, oga, cds)


DFT_HALO = 16


def _dft_body(c_ref, sn_ref, xc_ref, xs_ref, og_ref, rev_ref, o_ref, *, scale):
    tk = o_ref.shape[1]
    a = _dot(c_ref[...], xc_ref[...])
    bn = _dot(sn_ref[...], xs_ref[...])
    o_ref[0] = _rms((a + bn)[:tk] * scale, og_ref[...]).astype(BF16)
    mirrored = _rms((a - bn) * scale, og_ref[...]).astype(BF16)
    o_ref[1] = _dot(rev_ref[...], mirrored).astype(BF16)


def _dft_call(c_half, sn_half, rev, xc, xs, ogb, seq):
    t, w = xc.shape
    nb = t // seq
    tk, rows = rev.shape
    nk = seq // (2 * tk)
    assert 2 * nk * tk == seq and c_half.shape[0] >= (nk - 1) * tk + rows
    scale = 1.0 / math.sqrt(seq * B_HEAD_DIM)
    lhs = pl.BlockSpec((pl.Element(rows), pl.Element(seq)), lambda k, b: (k * tk, 0))
    rhs = pl.BlockSpec((seq, w), lambda k, b: (b, 0))
    return pl.pallas_call(
        functools.partial(_dft_body, scale=scale),
        grid=(nk, nb),
        in_specs=[lhs, lhs, rhs, rhs, pl.BlockSpec((1, w), lambda k, b: (0, 0)), _const_spec(rev.shape)],
        out_specs=pl.BlockSpec((None, 2, None, tk, w), lambda k, b: (b, 0, k, 0, 0)),
        out_shape=jax.ShapeDtypeStruct((nb, 2, nk, tk, w), BF16),
        compiler_params=_params("parallel", "parallel"),
        name="fnet_seq_dft",
    )(c_half, sn_half, xc, xs, ogb, rev)


def _dft_half_matrices(n, tk):
    nk = n // (2 * tk)
    rows = tk + DFT_HALO
    r = 64
    n_a = -(-((nk - 1) * tk + rows) // r)
    s = jnp.arange(n, dtype=jnp.int32)[None, :]
    unit = 2.0 * math.pi / n
    alpha = (((r * jnp.arange(n_a, dtype=jnp.int32)[:, None]) * s) % n).astype(F32) * unit
    beta = ((jnp.arange(r, dtype=jnp.int32)[:, None] * s) % n).astype(F32) * unit
    ca, sa, cb, sb = jnp.cos(alpha), jnp.sin(alpha), jnp.cos(beta), jnp.sin(beta)
    cos = (ca[:, None, :] * cb[None, :, :] - sa[:, None, :] * sb[None, :, :]).reshape(n_a * r, n).astype(BF16)
    nsin = (-(sa[:, None, :] * cb[None, :, :] + ca[:, None, :] * sb[None, :, :])).reshape(n_a * r, n).astype(BF16)
    rev = (jnp.arange(rows)[None, :] == tk - jnp.arange(tk)[:, None]).astype(BF16)
    return cos, nsin, rev


def _s5_weights(lam_re, lam_im, log_step, b_re, b_im, c_re, c_im, d_skip):
    L = S5_CHUNK
    G, P = lam_re.shape[1:]
    step = jnp.exp(log_step)[..., None]
    lr, li = lam_re, lam_im
    mag = jnp.exp(lr * step)
    ab_re, ab_im = mag * jnp.cos(li * step), mag * jnp.sin(li * step)
    den = lr * lr + li * li
    nr = ab_re - 1.0
    q_re = (nr * lr + ab_im * li) / den
    q_im = (ab_im * lr - nr * li) / den
    bb_re = q_re[..., None] * b_re[None] - q_im[..., None] * b_im[None]
    bb_im = q_re[..., None] * b_im[None] + q_im[..., None] * b_re[None]
    k = jnp.arange(L + 1, dtype=F32)[:, None, None, None]
    pmag = jnp.exp(k * (lr * step))
    pw_re, pw_im = pmag * jnp.cos(k * (li * step)), pmag * jnp.sin(k * (li * step))
    w_re = pw_re[..., None] * bb_re - pw_im[..., None] * bb_im
    w_im = pw_re[..., None] * bb_im + pw_im[..., None] * bb_re
    kern = (jnp.einsum('gop,ktgpc->ktgoc', c_re, w_re[:L]) - jnp.einsum('gop,ktgpc->ktgoc', c_im, w_im[:L]))
    centre = kern[0, 0] + kern[0, 1] + jnp.eye(C_GROUP) * d_skip.reshape(G, 1, C_GROUP)
    lags = jnp.concatenate([kern[1:, 1][::-1], centre[None], kern[1:, 0]], axis=0)
    lags = lags.transpose(1, 3, 0, 2).reshape(G, C_GROUP, (2 * L - 1) * C_GROUP)
    lags = jnp.pad(lags, ((0, 0), (0, 0), (0, C_GROUP)))
    to_rows = lambda a: a.transpose(1, 0, 3, 2).reshape(G, S5_COLS, P)
    p_mat = jnp.concatenate([to_rows(w_re[:L, 0][::-1]), to_rows(w_re[:L, 1]),
                             to_rows(w_im[:L, 0][::-1]), to_rows(w_im[:L, 1])], axis=-1)
    def readout(pr, pi):
        wr = c_re[None] * pr[:, :, None, :] - c_im[None] * pi[:, :, None, :]
        wi = c_re[None] * pi[:, :, None, :] + c_im[None] * pr[:, :, None, :]
        fl = lambda a: a.transpose(1, 3, 0, 2).reshape(G, P, S5_COLS)
        return fl(wr), fl(-wi)
    qf_re, qf_im = readout(pw_re[1:, 0], pw_im[1:, 0])
    qb_re, qb_im = readout(pw_re[1:, 1][::-1], pw_im[1:, 1][::-1])
    z = jnp.zeros_like(qf_re)
    qf = jnp.concatenate([qf_re, z, qf_im, z], axis=1)
    qb = jnp.concatenate([z, qb_re, z, qb_im], axis=1)
    q_mat = jnp.concatenate([qf, qb], axis=1)
    a_re = jnp.concatenate([pw_re[L, 0], pw_re[L, 1]], axis=-1)
    a_im = jnp.concatenate([pw_im[L, 0], pw_im[L, 1]], axis=-1)
    return p_mat.astype(BF16), lags, q_mat.astype(BF16), a_re, a_im


GROUPS_PER_BLOCK = LANES // C_GROUP


def _s5_body(z_ref, p_ref, lag_ref, q_ref, are_ref, aim_ref, o_ref,
             m_scr, v_scr, u_scr, s_scr, ha_scr, hb_scr, y_scr):
    nchunks = z_ref.shape[1] // S5_CHUNK
    rows = GROUPS_PER_BLOCK
    n = nchunks * rows

    @pl.when(pl.program_id(1) == 0)
    def _():
        width = lag_ref.shape[2]
        for g in range(rows):
            lag = lag_ref[g]
            for i in range(S5_CHUNK):
                start = (S5_CHUNK - 1 - i) * C_GROUP
                window = lag if start == 0 else pltpu.roll(lag, width - start, axis=1)
                m_scr[g, i * C_GROUP:(i + 1) * C_GROUP, :] = window[:, :S5_COLS].astype(BF16)

    for i in range(S5_CHUNK):
        t = z_ref[0, pl.ds(i, nchunks, stride=S5_CHUNK), :].T
        for g in range(rows):
            v_scr[g, i * C_GROUP:(i + 1) * C_GROUP, :] = t[g * C_GROUP:(g + 1) * C_GROUP, :]
    for g in range(rows):
        u = v_scr[g].T.astype(BF16)
        u_scr[g] = u
        s = _dot(u, p_ref[g])
        for part in range(2):
            s_scr[part, pl.ds(g, nchunks, stride=rows), :] = s[:, part * LANES:(part + 1) * LANES]

    ar = are_ref[...]
    ai = aim_ref[...]
    is_fwd = lax.broadcasted_iota(jnp.int32, (rows, LANES), 1) < C_STATE
    zero = jnp.zeros((rows, LANES), F32)
    for part in range(2):
        ha_scr[part, 0:rows, :] = zero
        hb_scr[part, n - rows:n, :] = zero

    def step(c, carry):
        hre, him = carry
        r_f = pl.multiple_of(c * rows, rows)
        r_b = pl.multiple_of((nchunks - 1 - c) * rows, rows)
        sre = jnp.where(is_fwd, s_scr[0, pl.ds(r_f, rows), :], s_scr[0, pl.ds(r_b, rows), :])
        sim = jnp.where(is_fwd, s_scr[1, pl.ds(r_f, rows), :], s_scr[1, pl.ds(r_b, rows), :])
        nre = hre * ar - him * ai + sre
        nim = hre * ai + him * ar + sim
        for part, h in enumerate((nre, nim)):
            ha_scr[part, pl.ds(pl.multiple_of(r_f + rows, rows), rows), :] = h
            hb_scr[part, pl.ds(pl.multiple_of(r_b - rows, rows), rows), :] = h
        return nre, nim

    lax.fori_loop(0, nchunks - 1, step, (zero, zero), unroll=5)

    def group_rows(scr, g):
        return jnp.concatenate([scr[part, pl.ds(g, nchunks, stride=rows), :] for part in range(2)],
                               axis=1).astype(BF16)

    for g in range(rows):
        ha = group_rows(ha_scr, g)
        hb = group_rows(hb_scr, g)
        y = (_dot(u_scr[g], m_scr[g]) + _dot(ha, q_ref[g, 0:S5_COLS, :])
             + _dot(hb, q_ref[g, S5_COLS:2 * S5_COLS, :]))
        t = y.T
        for j in range(S5_CHUNK):
            y_scr[j, g * C_GROUP:(g + 1) * C_GROUP, :] = t[j * C_GROUP:(j + 1) * C_GROUP, :]
    for j in range(S5_CHUNK):
        o_ref[0, pl.ds(j, nchunks, stride=S5_CHUNK), :] = y_scr[j].T


def _s5_mixer(zc, nb, seq, weights, layer):
    p_mat, lags, q_mat, a_re, a_im = weights
    width = zc.shape[1]
    nchunks = seq // S5_CHUNK
    gb = GROUPS_PER_BLOCK
    per_block = lambda a: pl.BlockSpec((None, gb) + a.shape[2:], lambda b, n: (layer, b) + (0,) * (a.ndim - 2))
    seq_block = pl.BlockSpec((1, seq, LANES), lambda b, n: (n, 0, b))
    return pl.pallas_call(
        _s5_body,
        grid=(width // LANES, nb),
        in_specs=[seq_block, per_block(p_mat), per_block(lags), per_block(q_mat), per_block(a_re),
                  per_block(a_im)],
        out_specs=seq_block,
        out_shape=jax.ShapeDtypeStruct((nb, seq, width), F32),
        scratch_shapes=[pltpu.VMEM((gb, S5_COLS, S5_COLS), BF16),
                        pltpu.VMEM((gb, S5_COLS, nchunks), F32), pltpu.VMEM((gb, nchunks, S5_COLS), BF16),
                        pltpu.VMEM((2, nchunks * gb, LANES), F32), pltpu.VMEM((2, nchunks * gb, LANES), F32),
                        pltpu.VMEM((2, nchunks * gb, LANES), F32), pltpu.VMEM((S5_CHUNK, LANES, nchunks), F32)],
        compiler_params=_params("parallel", "arbitrary"),
        name="s5_mixer",
    )(zc.reshape(nb, seq, width), p_mat, lags, q_mat, a_re, a_im).reshape(nb * seq, width)


def _out_body(*refs, tile_bounds):
    nx = len(tile_bounds)
    x_refs = refs[:nx]
    ya_ref, yb_ref, yc_ref, wg_ref, bg_ref, ogc_ref, wo_ref, o_ref = refs[nx:]
    acc = _select_rows(x_refs, tile_bounds) + _dot(ya_ref[...], wo_ref[0:A_WIDTH, :])
    acc = acc + _dot(yb_ref[...], wo_ref[A_WIDTH:A_WIDTH + B_WIDTH, :])
    y = _gelu(yc_ref[...].astype(F32))
    yc = y * _sigmoid(_dot(y.astype(BF16), wg_ref[...]) + bg_ref[...])
    yc = _rms(yc, ogc_ref[...]).astype(BF16)
    o_ref[...] = acc + _dot(yc, wo_ref[A_WIDTH + B_WIDTH:, :])


def _out_call(xs, ya, yb, yc, w_glu, b_glu, ogc, w_out, layer):
    t, cw = yc.shape
    d = w_out.shape[2]
    tm = TOKEN_TILE
    x_specs, tile_bounds = _row_tile_specs(xs, tm)
    assert tile_bounds[-1] * tm == t
    row = lambda w: pl.BlockSpec((tm, w), lambda i: (i, 0))
    _, _, nk, tk, bw = yb.shape
    assert tk == tm

    def yb_index(i):
        j = i % (2 * nk)
        return (i // (2 * nk), j // nk, jnp.where(j < nk, j, 2 * nk - 1 - j), 0, 0)

    return pl.pallas_call(
        functools.partial(_out_body, tile_bounds=tile_bounds),
        grid=(t // tm,),
        in_specs=x_specs + [row(A_WIDTH), pl.BlockSpec((None, None, None, tm, bw), yb_index), row(cw),
                            _const_spec((cw, cw), layer),
                            _const_spec((1, cw)), _const_spec((1, cw)), _const_spec(w_out.shape[1:], layer)],
        out_specs=row(d),
        out_shape=jax.ShapeDtypeStruct((t, d), F32),
        compiler_params=_params("parallel"),
        name="glu_out_proj",
    )(*xs, ya, yb, yc, w_glu, b_glu, ogc, w_out)


def _ffn_body(x_ref, wg_ref, wu_ref, wd_ref, gf_ref, o_ref, h_scr, inv_scr, *, final_norm):
    f = pl.program_id(1)

    def ff_tile(h, inv):
        gate = _dot(h, wg_ref[...]) * inv
        a = (gate * _sigmoid(gate) * (_dot(h, wu_ref[...]) * inv)).astype(BF16)
        return _dot(a, wd_ref[...])

    @pl.when(f == 0)
    def _():
        x = x_ref[...]
        h = x.astype(BF16)
        inv = lax.rsqrt(jnp.mean(x * x, axis=-1, keepdims=True) + EPS)
        h_scr[...] = h
        inv_scr[...] = jnp.broadcast_to(inv, inv_scr.shape)
        o_ref[...] = x + ff_tile(h, inv)

    @pl.when(f > 0)
    def _():
        o_ref[...] += ff_tile(h_scr[...], inv_scr[:, 0:1])

    if final_norm:
        @pl.when(f == pl.num_programs(1) - 1)
        def _():
            o_ref[...] = _rms(o_ref[...], gf_ref[...])


def _ffn_call(x, w_gate, w_up, w_down, layer, gf, final_norm, row_start=0, n_rows=None):
    d = x.shape[1]
    ff = w_gate.shape[2]
    n_rows = x.shape[0] if n_rows is None else n_rows
    tm = min(FFN_TOKEN_TILE, n_rows)
    tf = FF_TILE
    assert n_rows % tm == 0 and row_start % tm == 0 and ff % tf == 0
    first = row_start // tm
    return pl.pallas_call(
        functools.partial(_ffn_body, final_norm=final_norm),
        grid=(n_rows // tm, ff // tf),
        in_specs=[pl.BlockSpec((tm, d), lambda i, f: (i + first, 0)),
                  pl.BlockSpec((None, d, tf), lambda i, f: (layer, 0, f)),
                  pl.BlockSpec((None, d, tf), lambda i, f: (layer, 0, f)),
                  pl.BlockSpec((None, tf, d), lambda i, f: (layer, f, 0)), _const_spec((1, d))],
        out_specs=pl.BlockSpec((tm, d), lambda i, f: (i, 0)),
        out_shape=jax.ShapeDtypeStruct((n_rows, d), F32),
        scratch_shapes=[pltpu.VMEM((tm, d), BF16), pltpu.VMEM((tm, LANES), F32)],
        compiler_params=_params("parallel", "arbitrary"),
        name="swiglu_ffn",
    )(x, w_gate, w_up, w_down, gf)


def _dft_matrices(n):
    s = jnp.arange(n, dtype=jnp.int32)[None, :]
    unit = 2.0 * math.pi / n
    r = math.isqrt(n)
    if r * r != n:
        ang = ((jnp.arange(n, dtype=jnp.int32)[:, None] * s) % n).astype(F32) * unit
        return jnp.cos(ang), -jnp.sin(ang)
    ab = jnp.arange(r, dtype=jnp.int32)[:, None]
    alpha = (((r * ab) * s) % n).astype(F32) * unit
    beta = ((ab * s) % n).astype(F32) * unit
    ca, sa, cb, sb = jnp.cos(alpha), jnp.sin(alpha), jnp.cos(beta), jnp.sin(beta)
    cos = ca[:, None, :] * cb[None, :, :] - sa[:, None, :] * sb[None, :, :]
    sin = sa[:, None, :] * cb[None, :, :] + ca[:, None, :] * sb[None, :, :]
    return cos.reshape(n, n), -sin.reshape(n, n)


def kernel(x_prompt, x_sample, norm1_g, w_in, a_v_g, a_ws, a_bs, c_lam_re, c_lam_im, c_log_step, c_b_re, c_b_im,
           c_c_re, c_c_im, c_d, c_w_glu, c_b_glu, out_norm_g, w_out, norm2_g, w_gate, w_up, w_down, final_g):
    seq, d = x_prompt.shape[1:]
    assert x_sample.shape[1:] == (seq, d)
    nb = x_prompt.shape[0] + x_sample.shape[0]
    n_prompt = x_prompt.shape[0] * seq
    depth = w_in.shape[0]
    xs = (x_prompt.reshape(-1, d), x_sample.reshape(-1, d))

    c_half, sn_half, rev = _dft_half_matrices(seq, min(DFT_ROW_TILE, seq // 2))
    cd, sdn = _dft_matrices(B_HEAD_DIM)
    cds = jnp.concatenate([cd, -sdn], axis=1).astype(BF16)
    row = lambda v: v.reshape(1, -1)
    g1, g2 = norm1_g[:, :, None], norm2_g[:, :, None]
    w_in, w_out, c_w_glu, w_gate, w_up, w_down = (
        w.astype(BF16) for w in (w_in * g1, w_out, c_w_glu, w_gate * g2, w_up * g2, w_down))
    s5w = jax.vmap(_s5_weights)(c_lam_re, c_lam_im, c_log_step, c_b_re, c_b_im, c_c_re, c_c_im, c_d)

    ws = a_ws.astype(BF16)
    bs = jnp.broadcast_to(a_bs[..., None], a_bs.shape + (A_HEAD_DIM,))

    for l in range(depth):
        og = out_norm_g[l]
        ya, xc, xsin, zc = _in_call(xs, w_in, l, row(a_v_g[l]), ws, bs, row(og[:A_WIDTH]), cds)
        yb = _dft_call(c_half, sn_half, rev, xc, xsin, row(og[A_WIDTH:A_WIDTH + B_WIDTH]), seq)
        yc = _s5_mixer(zc, nb, seq, s5w, l)
        x = _out_call(xs, ya, yb, yc, c_w_glu, row(c_b_glu[l]), row(og[A_WIDTH + B_WIDTH:]), w_out, l)
        ffn = functools.partial(_ffn_call, x, w_gate, w_up, w_down, l, row(final_g))
        if l < depth - 1:
            xs = (ffn(final_norm=False),)
        else:
            y_prompt = ffn(final_norm=True, row_start=0, n_rows=n_prompt)
            y_sample = ffn(final_norm=True, row_start=n_prompt, n_rows=x.shape[0] - n_prompt)

    return (y_prompt.reshape(x_prompt.shape), y_sample.reshape(x_sample.shape))
```

```python
import functools
import math

import jax
import jax.numpy as jnp
from jax import lax
from jax.experimental import pallas as pl
from jax.experimental.pallas import tpu as pltpu

F32 = jnp.float32
BF16 = jnp.bfloat16

EPS = 1e-6
A_HEADS = 6
A_HEAD_DIM = 128
A_WIDTH = A_HEADS * A_HEAD_DIM
CHUNK = 128
B_HEADS = 4
B_HEAD_DIM = 128
B_WIDTH = B_HEADS * B_HEAD_DIM
C_GROUP = 16
C_STATE = 64

LANES = 128
SUBLANES = 8
S5_CHUNK = 16
S5_COLS = S5_CHUNK * C_GROUP
VMEM_LIMIT_BYTES = 56 * 1024 * 1024

TOKEN_TILE = 512
FFN_TOKEN_TILE = 1024
FF_TILE = 512
DFT_ROW_TILE = 512
GELU_C = math.sqrt(2.0 / math.pi)


def _gelu(x):
    return 0.5 * x * (1.0 + jnp.tanh(GELU_C * (x + 0.044715 * (x * x * x))))


def _sigmoid(x):
    return 1.0 / (1.0 + jnp.exp(-x))


def _rms(x, g):
    return x * lax.rsqrt(jnp.mean(x * x, axis=-1, keepdims=True) + EPS) * g


def _dot(a, b):
    return jnp.dot(a, b, preferred_element_type=F32)


def _const_spec(shape, layer=None):
    nd = len(shape)
    if layer is None:
        return pl.BlockSpec(shape, lambda *_: (0,) * nd, pipeline_mode=pl.Buffered(1))
    return pl.BlockSpec((None,) + tuple(shape), lambda *_: (layer,) + (0,) * nd, pipeline_mode=pl.Buffered(1))


def _row_tile_specs(xs, tm):
    specs, bounds, start = [], [], 0
    for x in xs:
        n = x.shape[0] // tm
        assert n * tm == x.shape[0] and tm % CHUNK == 0, (x.shape, tm)
        specs.append(pl.BlockSpec((tm, x.shape[1]),
                                  lambda i, *_, start=start, n=n: (jnp.clip(i - start, 0, n - 1), 0)))
        start += n
        bounds.append(start)
    return specs, tuple(bounds)


def _select_rows(x_refs, bounds):
    x = x_refs[-1][...]
    for k in range(len(x_refs) - 2, -1, -1):
        x = jnp.where(pl.program_id(0) < bounds[k], x_refs[k][...], x)
    return x


def _params(*sem):
    return pltpu.CompilerParams(dimension_semantics=sem, vmem_limit_bytes=VMEM_LIMIT_BYTES)


def _in_body(*refs, tile_bounds):
    nx = len(tile_bounds)
    x_refs = refs[:nx]
    (w_ref, vg_ref, ws_ref, bs_ref, oga_ref, cds_ref,
     ya_ref, xc_ref, xs_ref, zc_ref, z0_scr, z1_scr, ya_scr) = refs[nx:]
    tm = ya_ref.shape[0]
    nch = tm // CHUNK
    i = pl.program_id(0)

    def step(z_new, z_old):
        x = _select_rows(x_refs, tile_bounds)
        h = (x * lax.rsqrt(jnp.mean(x * x, axis=-1, keepdims=True) + EPS)).astype(BF16)
        zb = _dot(h, w_ref[:, 2 * A_WIDTH:2 * A_WIDTH + B_WIDTH]).astype(BF16)
        slab = 2 * A_WIDTH // A_HEADS
        for hd in range(A_HEADS):
            z_new[:, hd * slab:(hd + 1) * slab] = _dot(h, w_ref[:, hd * slab:(hd + 1) * slab])
            lo = hd * A_HEAD_DIM
            u = _gelu(z_old[:, lo:lo + A_HEAD_DIM])
            v = _gelu(z_old[:, A_WIDTH + lo:A_WIDTH + lo + A_HEAD_DIM])
            v = _rms(v, vg_ref[:, lo:lo + A_HEAD_DIM]).astype(BF16)
            v_wide = jnp.concatenate([v[c * CHUNK:(c + 1) * CHUNK, :] for c in range(nch)], axis=1)
            mixed = _dot(ws_ref[hd], v_wide)
            for c in range(nch):
                r = c * CHUNK
                ya_scr[r:r + CHUNK, lo:lo + A_HEAD_DIM] = u[r:r + CHUNK, :] * (
                    mixed[:, c * A_HEAD_DIM:(c + 1) * A_HEAD_DIM] + bs_ref[hd])
        zc_ref[...] = _dot(h, w_ref[:, 2 * A_WIDTH + B_WIDTH:])
        for g in range(B_HEADS):
            lo = g * B_HEAD_DIM
            t = _dot(zb[:, lo:lo + B_HEAD_DIM], cds_ref[...])
            xc_ref[:, lo:lo + B_HEAD_DIM] = t[:, :B_HEAD_DIM].astype(BF16)
            xs_ref[:, lo:lo + B_HEAD_DIM] = t[:, B_HEAD_DIM:].astype(BF16)
        ya_ref[...] = _rms(ya_scr[...], oga_ref[...]).astype(BF16)

    @pl.when(i == 0)
    def _():
        z1_scr[...] = jnp.zeros(z1_scr.shape, F32)

    @pl.when(i % 2 == 0)
    def _():
        step(z0_scr, z1_scr)

    @pl.when(i % 2 == 1)
    def _():
        step(z1_scr, z0_scr)


def _in_call(xs, w_in, layer, vg, ws, bs, oga, cds):
    d = xs[0].shape[1]
    n_in = w_in.shape[2]
    c_width = n_in - 2 * A_WIDTH - B_WIDTH
    tm = TOKEN_TILE
    x_specs, tile_bounds = _row_tile_specs(xs, tm)
    nt = tile_bounds[-1]
    t = nt * tm
    this_tile = lambda w: pl.BlockSpec((tm, w), lambda i: (jnp.minimum(i, nt - 1), 0))
    prev_tile = lambda w: pl.BlockSpec((tm, w), lambda i: (jnp.maximum(i - 1, 0), 0))
    return pl.pallas_call(
        functools.partial(_in_body, tile_bounds=tile_bounds),
        grid=(nt + 1,),
        in_specs=x_specs + [_const_spec((d, n_in), layer), _const_spec((1, A_WIDTH)),
                            _const_spec(ws.shape[1:], layer), _const_spec(bs.shape[1:], layer),
                            _const_spec((1, A_WIDTH)),
                            _const_spec(cds.shape)],
        out_specs=[prev_tile(A_WIDTH), this_tile(B_WIDTH), this_tile(B_WIDTH), this_tile(c_width)],
        out_shape=[jax.ShapeDtypeStruct((t, A_WIDTH), BF16), jax.ShapeDtypeStruct((t, B_WIDTH), BF16),
                   jax.ShapeDtypeStruct((t, B_WIDTH), BF16), jax.ShapeDtypeStruct((t, c_width), F32)],
        scratch_shapes=[pltpu.VMEM((tm, 2 * A_WIDTH), F32), pltpu.VMEM((tm, 2 * A_WIDTH), F32),
                        pltpu.VMEM((tm, A_WIDTH), F32)],
        compiler_params=_params("arbitrary"),
        name="in_proj_gmlp",
    )(*xs, w_in, vg, ws, bs, oga, cds)


DFT_HALO = 16


def _dft_body(c_ref, sn_ref, xc_ref, xs_ref, og_ref, rev_ref, rev0_ref, o_ref, xcf_scr, xsf_scr, *, scale):
    tk = o_ref.shape[1]
    rows = c_ref.shape[0]
    half = xcf_scr.shape[0]
    n = xc_ref.shape[0]

    @pl.when(pl.program_id(1) == 0)
    def _():
        for m in range(half // tk):
            lo = m * tk
            start, sel = (n - rows, rev0_ref) if m == 0 else (n - lo - tk, rev_ref)
            pc = _dot(sel[...], xc_ref[start:start + rows, :])
            ps = _dot(sel[...], xs_ref[start:start + rows, :])
            xcf_scr[lo:lo + tk, :] = (xc_ref[lo:lo + tk, :].astype(F32) + pc).astype(BF16)
            xsf_scr[lo:lo + tk, :] = (xs_ref[lo:lo + tk, :].astype(F32) - ps).astype(BF16)

    sign = 1.0 - 2.0 * (lax.broadcasted_iota(jnp.int32, (rows, 1), 0) % 2).astype(F32)
    a = _dot(c_ref[...], xcf_scr[...]) + sign * xc_ref[half:half + 1, :].astype(F32)
    bn = _dot(sn_ref[...], xsf_scr[...])
    o_ref[0] = _rms((a + bn)[:tk] * scale, og_ref[...]).astype(BF16)
    mirrored = _rms((a - bn) * scale, og_ref[...]).astype(BF16)
    o_ref[1] = _dot(rev_ref[...], mirrored).astype(BF16)


def _dft_call(c_half, sn_half, rev, rev0, xc, xs, ogb, seq):
    t, w = xc.shape
    nb = t // seq
    tk, rows = rev.shape
    half = seq // 2
    nk = half // tk
    assert nk * tk == half and tk % 2 == 0 and c_half.shape[0] >= (nk - 1) * tk + rows
    scale = 1.0 / math.sqrt(seq * B_HEAD_DIM)
    lhs = pl.BlockSpec((pl.Element(rows), pl.Element(half)), lambda b, k: (k * tk, 0))
    rhs = pl.BlockSpec((seq, w), lambda b, k: (b, 0))
    return pl.pallas_call(
        functools.partial(_dft_body, scale=scale),
        grid=(nb, nk),
        in_specs=[lhs, lhs, rhs, rhs, pl.BlockSpec((1, w), lambda b, k: (0, 0)), _const_spec(rev.shape),
                  _const_spec(rev0.shape)],
        out_specs=pl.BlockSpec((None, 2, None, tk, w), lambda b, k: (b, 0, k, 0, 0)),
        out_shape=jax.ShapeDtypeStruct((nb, 2, nk, tk, w), BF16),
        scratch_shapes=[pltpu.VMEM((half, w), BF16), pltpu.VMEM((half, w), BF16)],
        compiler_params=_params("parallel", "arbitrary"),
        name="fnet_seq_dft",
    )(c_half, sn_half, xc, xs, ogb, rev, rev0)


def _dft_half_matrices(n, tk):
    nk = n // (2 * tk)
    rows = tk + DFT_HALO
    r = 64
    n_a = -(-((nk - 1) * tk + rows) // r)
    s = jnp.arange(n, dtype=jnp.int32)[None, :]
    unit = 2.0 * math.pi / n
    alpha = (((r * jnp.arange(n_a, dtype=jnp.int32)[:, None]) * s) % n).astype(F32) * unit
    beta = ((jnp.arange(r, dtype=jnp.int32)[:, None] * s) % n).astype(F32) * unit
    ca, sa, cb, sb = jnp.cos(alpha), jnp.sin(alpha), jnp.cos(beta), jnp.sin(beta)
    cos = (ca[:, None, :] * cb[None, :, :] - sa[:, None, :] * sb[None, :, :]).reshape(n_a * r, n).astype(BF16)
    nsin = (-(sa[:, None, :] * cb[None, :, :] + ca[:, None, :] * sb[None, :, :])).reshape(n_a * r, n).astype(BF16)
    col, i = jnp.arange(rows)[None, :], jnp.arange(tk)[:, None]
    return cos, nsin, (col == tk - i).astype(BF16), (col == rows - i).astype(BF16)


def _s5_weights(lam_re, lam_im, log_step, b_re, b_im, c_re, c_im, d_skip):
    L = S5_CHUNK
    G, P = lam_re.shape[1:]
    step = jnp.exp(log_step)[..., None]
    lr, li = lam_re, lam_im
    mag = jnp.exp(lr * step)
    ab_re, ab_im = mag * jnp.cos(li * step), mag * jnp.sin(li * step)
    den = lr * lr + li * li
    nr = ab_re - 1.0
    q_re = (nr * lr + ab_im * li) / den
    q_im = (ab_im * lr - nr * li) / den
    bb_re = q_re[..., None] * b_re[None] - q_im[..., None] * b_im[None]
    bb_im = q_re[..., None] * b_im[None] + q_im[..., None] * b_re[None]
    k = jnp.arange(L + 1, dtype=F32)[:, None, None, None]
    pmag = jnp.exp(k * (lr * step))
    pw_re, pw_im = pmag * jnp.cos(k * (li * step)), pmag * jnp.sin(k * (li * step))
    w_re = pw_re[..., None] * bb_re - pw_im[..., None] * bb_im
    w_im = pw_re[..., None] * bb_im + pw_im[..., None] * bb_re
    kern = (jnp.einsum('gop,ktgpc->ktgoc', c_re, w_re[:L]) - jnp.einsum('gop,ktgpc->ktgoc', c_im, w_im[:L]))
    centre = kern[0, 0] + kern[0, 1] + jnp.eye(C_GROUP) * d_skip.reshape(G, 1, C_GROUP)
    lags = jnp.concatenate([kern[1:, 1][::-1], centre[None], kern[1:, 0]], axis=0)
    lags = lags.transpose(1, 3, 0, 2).reshape(G, C_GROUP, (2 * L - 1) * C_GROUP)
    lags = jnp.pad(lags, ((0, 0), (0, 0), (0, C_GROUP)))
    to_rows = lambda a: a.transpose(1, 0, 3, 2).reshape(G, S5_COLS, P)
    p_mat = jnp.concatenate([to_rows(w_re[:L, 0][::-1]), to_rows(w_re[:L, 1]),
                             to_rows(w_im[:L, 0][::-1]), to_rows(w_im[:L, 1])], axis=-1)
    def readout(pr, pi):
        wr = c_re[None] * pr[:, :, None, :] - c_im[None] * pi[:, :, None, :]
        wi = c_re[None] * pi[:, :, None, :] + c_im[None] * pr[:, :, None, :]
        fl = lambda a: a.transpose(1, 3, 0, 2).reshape(G, P, S5_COLS)
        return fl(wr), fl(-wi)
    qf_re, qf_im = readout(pw_re[1:, 0], pw_im[1:, 0])
    qb_re, qb_im = readout(pw_re[1:, 1][::-1], pw_im[1:, 1][::-1])
    z = jnp.zeros_like(qf_re)
    qf = jnp.concatenate([qf_re, z, qf_im, z], axis=1)
    qb = jnp.concatenate([z, qb_re, z, qb_im], axis=1)
    q_mat = jnp.concatenate([qf, qb], axis=1)
    a_re = jnp.concatenate([pw_re[L, 0], pw_re[L, 1]], axis=-1)
    a_im = jnp.concatenate([pw_im[L, 0], pw_im[L, 1]], axis=-1)
    return p_mat.astype(BF16), lags, q_mat.astype(BF16), a_re, a_im


GROUPS_PER_BLOCK = LANES // C_GROUP


def _s5_body(z_ref, p_ref, lag_ref, q_ref, are_ref, aim_ref, o_ref,
             m_scr, v_scr, u_scr, s_scr, ha_scr, hb_scr, y_scr):
    nchunks = z_ref.shape[1] // S5_CHUNK
    rows = GROUPS_PER_BLOCK
    n = nchunks * rows

    @pl.when(pl.program_id(1) == 0)
    def _():
        width = lag_ref.shape[2]
        for g in range(rows):
            lag = lag_ref[g]
            for i in range(S5_CHUNK):
                start = (S5_CHUNK - 1 - i) * C_GROUP
                window = lag if start == 0 else pltpu.roll(lag, width - start, axis=1)
                m_scr[g, i * C_GROUP:(i + 1) * C_GROUP, :] = window[:, :S5_COLS].astype(BF16)

    for i in range(S5_CHUNK):
        t = z_ref[0, pl.ds(i, nchunks, stride=S5_CHUNK), :].T
        for g in range(rows):
            v_scr[g, i * C_GROUP:(i + 1) * C_GROUP, :] = t[g * C_GROUP:(g + 1) * C_GROUP, :]
    for g in range(rows):
        u = v_scr[g].T.astype(BF16)
        u_scr[g] = u
        s = _dot(u, p_ref[g])
        for part in range(2):
            s_scr[part, pl.ds(g, nchunks, stride=rows), :] = s[:, part * LANES:(part + 1) * LANES]

    ar = are_ref[...]
    ai = aim_ref[...]
    is_fwd = lax.broadcasted_iota(jnp.int32, (rows, LANES), 1) < C_STATE
    zero = jnp.zeros((rows, LANES), F32)
    for part in range(2):
        ha_scr[part, 0:rows, :] = zero
        hb_scr[part, n - rows:n, :] = zero

    def step(c, carry):
        hre, him = carry
        r_f = pl.multiple_of(c * rows, rows)
        r_b = pl.multiple_of((nchunks - 1 - c) * rows, rows)
        sre = jnp.where(is_fwd, s_scr[0, pl.ds(r_f, rows), :], s_scr[0, pl.ds(r_b, rows), :])
        sim = jnp.where(is_fwd, s_scr[1, pl.ds(r_f, rows), :], s_scr[1, pl.ds(r_b, rows), :])
        nre = hre * ar - him * ai + sre
        nim = hre * ai + him * ar + sim
        for part, h in enumerate((nre, nim)):
            ha_scr[part, pl.ds(pl.multiple_of(r_f + rows, rows), rows), :] = h
            hb_scr[part, pl.ds(pl.multiple_of(r_b - rows, rows), rows), :] = h
        return nre, nim

    lax.fori_loop(0, nchunks - 1, step, (zero, zero), unroll=5)

    def group_rows(scr, g):
        return jnp.concatenate([scr[part, pl.ds(g, nchunks, stride=rows), :] for part in range(2)],
                               axis=1).astype(BF16)

    for g in range(rows):
        ha = group_rows(ha_scr, g)
        hb = group_rows(hb_scr, g)
        y = (_dot(u_scr[g], m_scr[g]) + _dot(ha, q_ref[g, 0:S5_COLS, :])
             + _dot(hb, q_ref[g, S5_COLS:2 * S5_COLS, :]))
        t = y.T
        for j in range(S5_CHUNK):
            y_scr[j, g * C_GROUP:(g + 1) * C_GROUP, :] = t[j * C_GROUP:(j + 1) * C_GROUP, :]
    for j in range(S5_CHUNK):
        o_ref[0, pl.ds(j, nchunks, stride=S5_CHUNK), :] = y_scr[j].T


def _s5_mixer(zc, nb, seq, weights, layer):
    p_mat, lags, q_mat, a_re, a_im = weights
    width = zc.shape[1]
    nchunks = seq // S5_CHUNK
    gb = GROUPS_PER_BLOCK
    per_block = lambda a: pl.BlockSpec((None, gb) + a.shape[2:], lambda b, n: (layer, b) + (0,) * (a.ndim - 2))
    seq_block = pl.BlockSpec((1, seq, LANES), lambda b, n: (n, 0, b))
    return pl.pallas_call(
        _s5_body,
        grid=(width // LANES, nb),
        in_specs=[seq_block, per_block(p_mat), per_block(lags), per_block(q_mat), per_block(a_re),
                  per_block(a_im)],
        out_specs=seq_block,
        out_shape=jax.ShapeDtypeStruct((nb, seq, width), F32),
        scratch_shapes=[pltpu.VMEM((gb, S5_COLS, S5_COLS), BF16),
                        pltpu.VMEM((gb, S5_COLS, nchunks), F32), pltpu.VMEM((gb, nchunks, S5_COLS), BF16),
                        pltpu.VMEM((2, nchunks * gb, LANES), F32), pltpu.VMEM((2, nchunks * gb, LANES), F32),
                        pltpu.VMEM((2, nchunks * gb, LANES), F32), pltpu.VMEM((S5_CHUNK, LANES, nchunks), F32)],
        compiler_params=_params("parallel", "arbitrary"),
        name="s5_mixer",
    )(zc.reshape(nb, seq, width), p_mat, lags, q_mat, a_re, a_im).reshape(nb * seq, width)


def _out_body(*refs, tile_bounds):
    nx = len(tile_bounds)
    x_refs = refs[:nx]
    ya_ref, yb_ref, yc_ref, wg_ref, bg_ref, ogc_ref, wo_ref, o_ref = refs[nx:]
    acc = _select_rows(x_refs, tile_bounds) + _dot(ya_ref[...], wo_ref[0:A_WIDTH, :])
    acc = acc + _dot(yb_ref[...], wo_ref[A_WIDTH:A_WIDTH + B_WIDTH, :])
    y = _gelu(yc_ref[...].astype(F32))
    yc = y * _sigmoid(_dot(y.astype(BF16), wg_ref[...]) + bg_ref[...])
    yc = _rms(yc, ogc_ref[...]).astype(BF16)
    o_ref[...] = acc + _dot(yc, wo_ref[A_WIDTH + B_WIDTH:, :])


def _out_call(xs, ya, yb, yc, w_glu, b_glu, ogc, w_out, layer):
    t, cw = yc.shape
    d = w_out.shape[2]
    tm = TOKEN_TILE
    x_specs, tile_bounds = _row_tile_specs(xs, tm)
    assert tile_bounds[-1] * tm == t
    row = lambda w: pl.BlockSpec((tm, w), lambda i: (i, 0))
    _, _, nk, tk, bw = yb.shape
    assert tk == tm

    def yb_index(i):
        j = i % (2 * nk)
        return (i // (2 * nk), j // nk, jnp.where(j < nk, j, 2 * nk - 1 - j), 0, 0)

    return pl.pallas_call(
        functools.partial(_out_body, tile_bounds=tile_bounds),
        grid=(t // tm,),
        in_specs=x_specs + [row(A_WIDTH), pl.BlockSpec((None, None, None, tm, bw), yb_index), row(cw),
                            _const_spec((cw, cw), layer),
                            _const_spec((1, cw)), _const_spec((1, cw)), _const_spec(w_out.shape[1:], layer)],
        out_specs=row(d),
        out_shape=jax.ShapeDtypeStruct((t, d), F32),
        compiler_params=_params("parallel"),
        name="glu_out_proj",
    )(*xs, ya, yb, yc, w_glu, b_glu, ogc, w_out)


def _ffn_body(x_ref, wg_ref, wu_ref, wd_ref, gf_ref, o_ref, h_scr, inv_scr, *, final_norm):
    f = pl.program_id(1)

    def ff_tile(h, inv):
        gate = _dot(h, wg_ref[...]) * inv
        a = (gate * _sigmoid(gate) * (_dot(h, wu_ref[...]) * inv)).astype(BF16)
        return _dot(a, wd_ref[...])

    @pl.when(f == 0)
    def _():
        x = x_ref[...]
        h = x.astype(BF16)
        inv = lax.rsqrt(jnp.mean(x * x, axis=-1, keepdims=True) + EPS)
        h_scr[...] = h
        inv_scr[...] = jnp.broadcast_to(inv, inv_scr.shape)
        o_ref[...] = x + ff_tile(h, inv)

    @pl.when(f > 0)
    def _():
        o_ref[...] += ff_tile(h_scr[...], inv_scr[:, 0:1])

    if final_norm:
        @pl.when(f == pl.num_programs(1) - 1)
        def _():
            o_ref[...] = _rms(o_ref[...], gf_ref[...])


def _ffn_call(x, w_gate, w_up, w_down, layer, gf, final_norm, row_start=0, n_rows=None):
    d = x.shape[1]
    ff = w_gate.shape[2]
    n_rows = x.shape[0] if n_rows is None else n_rows
    tm = min(FFN_TOKEN_TILE, n_rows)
    tf = FF_TILE
    assert n_rows % tm == 0 and row_start % tm == 0 and ff % tf == 0
    first = row_start // tm
    return pl.pallas_call(
        functools.partial(_ffn_body, final_norm=final_norm),
        grid=(n_rows // tm, ff // tf),
        in_specs=[pl.BlockSpec((tm, d), lambda i, f: (i + first, 0)),
                  pl.BlockSpec((None, d, tf), lambda i, f: (layer, 0, f)),
                  pl.BlockSpec((None, d, tf), lambda i, f: (layer, 0, f)),
                  pl.BlockSpec((None, tf, d), lambda i, f: (layer, f, 0)), _const_spec((1, d))],
        out_specs=pl.BlockSpec((tm, d), lambda i, f: (i, 0)),
        out_shape=jax.ShapeDtypeStruct((n_rows, d), F32),
        scratch_shapes=[pltpu.VMEM((tm, d), BF16), pltpu.VMEM((tm, LANES), F32)],
        compiler_params=_params("parallel", "arbitrary"),
        name="swiglu_ffn",
    )(x, w_gate, w_up, w_down, gf)


def _dft_matrices(n):
    s = jnp.arange(n, dtype=jnp.int32)[None, :]
    unit = 2.0 * math.pi / n
    r = math.isqrt(n)
    if r * r != n:
        ang = ((jnp.arange(n, dtype=jnp.int32)[:, None] * s) % n).astype(F32) * unit
        return jnp.cos(ang), -jnp.sin(ang)
    ab = jnp.arange(r, dtype=jnp.int32)[:, None]
    alpha = (((r * ab) * s) % n).astype(F32) * unit
    beta = ((ab * s) % n).astype(F32) * unit
    ca, sa, cb, sb = jnp.cos(alpha), jnp.sin(alpha), jnp.cos(beta), jnp.sin(beta)
    cos = ca[:, None, :] * cb[None, :, :] - sa[:, None, :] * sb[None, :, :]
    sin = sa[:, None, :] * cb[None, :, :] + ca[:, None, :] * sb[None, :, :]
    return cos.reshape(n, n), -sin.reshape(n, n)


def kernel(x_prompt, x_sample, norm1_g, w_in, a_v_g, a_ws, a_bs, c_lam_re, c_lam_im, c_log_step, c_b_re, c_b_im,
           c_c_re, c_c_im, c_d, c_w_glu, c_b_glu, out_norm_g, w_out, norm2_g, w_gate, w_up, w_down, final_g):
    seq, d = x_prompt.shape[1:]
    assert x_sample.shape[1:] == (seq, d)
    nb = x_prompt.shape[0] + x_sample.shape[0]
    n_prompt = x_prompt.shape[0] * seq
    depth = w_in.shape[0]
    xs = (x_prompt.reshape(-1, d), x_sample.reshape(-1, d))

    c_half, sn_half, rev, rev0 = _dft_half_matrices(seq, min(DFT_ROW_TILE, seq // 2))
    cd, sdn = _dft_matrices(B_HEAD_DIM)
    cds = jnp.concatenate([cd, -sdn], axis=1).astype(BF16)
    row = lambda v: v.reshape(1, -1)
    g1, g2 = norm1_g[:, :, None], norm2_g[:, :, None]
    w_in, w_out, c_w_glu, w_gate, w_up, w_down = (
        w.astype(BF16) for w in (w_in * g1, w_out, c_w_glu, w_gate * g2, w_up * g2, w_down))
    s5w = jax.vmap(_s5_weights)(c_lam_re, c_lam_im, c_log_step, c_b_re, c_b_im, c_c_re, c_c_im, c_d)

    ws = a_ws.astype(BF16)
    bs = jnp.broadcast_to(a_bs[..., None], a_bs.shape + (A_HEAD_DIM,))

    for l in range(depth):
        og = out_norm_g[l]
        ya, xc, xsin, zc = _in_call(xs, w_in, l, row(a_v_g[l]), ws, bs, row(og[:A_WIDTH]), cds)
        yb = _dft_call(c_half, sn_half, rev, rev0, xc, xsin, row(og[A_WIDTH:A_WIDTH + B_WIDTH]), seq)
        yc = _s5_mixer(zc, nb, seq, s5w, l)
        x = _out_call(xs, ya, yb, yc, c_w_glu, row(c_b_glu[l]), row(og[A_WIDTH + B_WIDTH:]), w_out, l)
        ffn = functools.partial(_ffn_call, x, w_gate, w_up, w_down, l, row(final_g))
        if l < depth - 1:
            xs = (ffn(final_norm=False),)
        else:
            y_prompt = ffn(final_norm=True, row_start=0, n_rows=n_prompt)
            y_sample = ffn(final_norm=True, row_start=n_prompt, n_rows=x.shape[0] - n_prompt)

    return (y_prompt.reshape(x_prompt.shape), y_sample.reshape(x_sample.shape))
```

```python
import functools
import math

import jax
import jax.numpy as jnp
from jax import lax
from jax.experimental import pallas as pl
from jax.experimental.pallas import tpu as pltpu

F32 = jnp.float32
BF16 = jnp.bfloat16

EPS = 1e-6
A_HEADS = 6
A_HEAD_DIM = 128
A_WIDTH = A_HEADS * A_HEAD_DIM
CHUNK = 128
B_HEADS = 4
B_HEAD_DIM = 128
B_WIDTH = B_HEADS * B_HEAD_DIM
C_GROUP = 16
C_STATE = 64

LANES = 128
V7X_MXU_WIDTH = 256
V7X_VMEM_BYTES = 64 * 1024 * 1024
VMEM_LIMIT_BYTES = V7X_VMEM_BYTES - 8 * 1024 * 1024
S5_CHUNK = V7X_MXU_WIDTH // C_GROUP
S5_COLS = S5_CHUNK * C_GROUP
DFT_TABLE_SPLIT = 64

TOKEN_TILE = 512
FFN_TOKEN_TILE = 1024
FF_TILE = 512
DFT_ROW_TILE = 512
GELU_C = math.sqrt(2.0 / math.pi)


def _gelu(x):
    return 0.5 * x * (1.0 + jnp.tanh(GELU_C * (x + 0.044715 * (x * x * x))))


def _sigmoid(x):
    return 1.0 / (1.0 + jnp.exp(-x))


def _rms(x, g):
    return x * lax.rsqrt(jnp.mean(x * x, axis=-1, keepdims=True) + EPS) * g


def _dot(a, b):
    return jnp.dot(a, b, preferred_element_type=F32)


def _const_spec(shape, layer=None):
    nd = len(shape)
    if layer is None:
        return pl.BlockSpec(shape, lambda *_: (0,) * nd, pipeline_mode=pl.Buffered(1))
    return pl.BlockSpec((None,) + tuple(shape), lambda *_: (layer,) + (0,) * nd, pipeline_mode=pl.Buffered(1))


def _row_tile_specs(xs, tm):
    specs, bounds, start = [], [], 0
    for x in xs:
        n = x.shape[0] // tm
        assert n * tm == x.shape[0] and tm % CHUNK == 0, (x.shape, tm)
        specs.append(pl.BlockSpec((tm, x.shape[1]),
                                  lambda i, *_, start=start, n=n: (jnp.clip(i - start, 0, n - 1), 0)))
        start += n
        bounds.append(start)
    return specs, tuple(bounds)


def _select_rows(x_refs, bounds):
    x = x_refs[-1][...]
    for k in range(len(x_refs) - 2, -1, -1):
        x = jnp.where(pl.program_id(0) < bounds[k], x_refs[k][...], x)
    return x


def _params(*sem):
    return pltpu.CompilerParams(dimension_semantics=sem, vmem_limit_bytes=VMEM_LIMIT_BYTES)


def _in_body(*refs, tile_bounds):
    nx = len(tile_bounds)
    x_refs = refs[:nx]
    (w_ref, vg_ref, ws_ref, bs_ref, oga_ref, cds_ref,
     ya_ref, xc_ref, xs_ref, zc_ref, z0_scr, z1_scr, ya_scr) = refs[nx:]
    tm = ya_ref.shape[0]
    nch = tm // CHUNK
    i = pl.program_id(0)

    def step(z_new, z_old):
        x = _select_rows(x_refs, tile_bounds)
        h = (x * lax.rsqrt(jnp.mean(x * x, axis=-1, keepdims=True) + EPS)).astype(BF16)
        zb = _dot(h, w_ref[:, 2 * A_WIDTH:2 * A_WIDTH + B_WIDTH]).astype(BF16)
        slab = 2 * A_WIDTH // A_HEADS
        for hd in range(A_HEADS):
            z_new[:, hd * slab:(hd + 1) * slab] = _dot(h, w_ref[:, hd * slab:(hd + 1) * slab])
            lo = hd * A_HEAD_DIM
            u = _gelu(z_old[:, lo:lo + A_HEAD_DIM])
            v = _gelu(z_old[:, A_WIDTH + lo:A_WIDTH + lo + A_HEAD_DIM])
            v = _rms(v, vg_ref[:, lo:lo + A_HEAD_DIM]).astype(BF16)
            v_wide = jnp.concatenate([v[c * CHUNK:(c + 1) * CHUNK, :] for c in range(nch)], axis=1)
            mixed = _dot(ws_ref[hd], v_wide)
            for c in range(nch):
                r = c * CHUNK
                ya_scr[r:r + CHUNK, lo:lo + A_HEAD_DIM] = u[r:r + CHUNK, :] * (
                    mixed[:, c * A_HEAD_DIM:(c + 1) * A_HEAD_DIM] + bs_ref[hd])
        zc_ref[...] = _dot(h, w_ref[:, 2 * A_WIDTH + B_WIDTH:])
        for g in range(B_HEADS):
            lo = g * B_HEAD_DIM
            t = _dot(zb[:, lo:lo + B_HEAD_DIM], cds_ref[...])
            xc_ref[:, lo:lo + B_HEAD_DIM] = t[:, :B_HEAD_DIM].astype(BF16)
            xs_ref[:, lo:lo + B_HEAD_DIM] = t[:, B_HEAD_DIM:].astype(BF16)
        ya_ref[...] = _rms(ya_scr[...], oga_ref[...]).astype(BF16)

    @pl.when(i == 0)
    def _():
        z1_scr[...] = jnp.zeros(z1_scr.shape, F32)

    @pl.when(i % 2 == 0)
    def _():
        step(z0_scr, z1_scr)

    @pl.when(i % 2 == 1)
    def _():
        step(z1_scr, z0_scr)


def _in_call(xs, w_in, layer, vg, ws, bs, oga, cds):
    d = xs[0].shape[1]
    n_in = w_in.shape[2]
    c_width = n_in - 2 * A_WIDTH - B_WIDTH
    tm = TOKEN_TILE
    x_specs, tile_bounds = _row_tile_specs(xs, tm)
    nt = tile_bounds[-1]
    t = nt * tm
    this_tile = lambda w: pl.BlockSpec((tm, w), lambda i: (jnp.minimum(i, nt - 1), 0))
    prev_tile = lambda w: pl.BlockSpec((tm, w), lambda i: (jnp.maximum(i - 1, 0), 0))
    return pl.pallas_call(
        functools.partial(_in_body, tile_bounds=tile_bounds),
        grid=(nt + 1,),
        in_specs=x_specs + [_const_spec((d, n_in), layer), _const_spec((1, A_WIDTH)),
                            _const_spec(ws.shape[1:], layer), _const_spec(bs.shape[1:], layer),
                            _const_spec((1, A_WIDTH)),
                            _const_spec(cds.shape)],
        out_specs=[prev_tile(A_WIDTH), this_tile(B_WIDTH), this_tile(B_WIDTH), this_tile(c_width)],
        out_shape=[jax.ShapeDtypeStruct((t, A_WIDTH), BF16), jax.ShapeDtypeStruct((t, B_WIDTH), BF16),
                   jax.ShapeDtypeStruct((t, B_WIDTH), BF16), jax.ShapeDtypeStruct((t, c_width), F32)],
        scratch_shapes=[pltpu.VMEM((tm, 2 * A_WIDTH), F32), pltpu.VMEM((tm, 2 * A_WIDTH), F32),
                        pltpu.VMEM((tm, A_WIDTH), F32)],
        compiler_params=_params("arbitrary"),
        name="in_proj_gmlp",
    )(*xs, w_in, vg, ws, bs, oga, cds)


DFT_HALO = 16


def _dft_body(c_ref, sn_ref, xc_ref, xs_ref, og_ref, rev_ref, rev0_ref, o_ref, xcf_scr, xsf_scr, *, scale):
    tk = o_ref.shape[1]
    rows = c_ref.shape[0]
    half = xcf_scr.shape[0]
    n = xc_ref.shape[0]

    @pl.when(pl.program_id(1) == 0)
    def _():
        for m in range(half // tk):
            lo = m * tk
            start, sel = (n - rows, rev0_ref) if m == 0 else (n - lo - tk, rev_ref)
            pc = _dot(sel[...], xc_ref[start:start + rows, :])
            ps = _dot(sel[...], xs_ref[start:start + rows, :])
            xcf_scr[lo:lo + tk, :] = (xc_ref[lo:lo + tk, :].astype(F32) + pc).astype(BF16)
            xsf_scr[lo:lo + tk, :] = (xs_ref[lo:lo + tk, :].astype(F32) - ps).astype(BF16)

    sign = 1.0 - 2.0 * (lax.broadcasted_iota(jnp.int32, (rows, 1), 0) % 2).astype(F32)
    a = _dot(c_ref[...], xcf_scr[...]) + sign * xc_ref[half:half + 1, :].astype(F32)
    bn = _dot(sn_ref[...], xsf_scr[...])
    o_ref[0] = _rms((a + bn)[:tk] * scale, og_ref[...]).astype(BF16)
    mirrored = _rms((a - bn) * scale, og_ref[...]).astype(BF16)
    o_ref[1] = _dot(rev_ref[...], mirrored).astype(BF16)


def _dft_call(c_half, sn_half, rev, rev0, xc, xs, ogb, seq):
    t, w = xc.shape
    nb = t // seq
    tk, rows = rev.shape
    half = seq // 2
    nk = half // tk
    assert nk * tk == half and tk % 2 == 0 and c_half.shape[0] >= (nk - 1) * tk + rows
    scale = 1.0 / math.sqrt(seq * B_HEAD_DIM)
    lhs = pl.BlockSpec((pl.Element(rows), pl.Element(half)), lambda b, k: (k * tk, 0))
    rhs = pl.BlockSpec((seq, w), lambda b, k: (b, 0))
    return pl.pallas_call(
        functools.partial(_dft_body, scale=scale),
        grid=(nb, nk),
        in_specs=[lhs, lhs, rhs, rhs, pl.BlockSpec((1, w), lambda b, k: (0, 0)), _const_spec(rev.shape),
                  _const_spec(rev0.shape)],
        out_specs=pl.BlockSpec((None, 2, None, tk, w), lambda b, k: (b, 0, k, 0, 0)),
        out_shape=jax.ShapeDtypeStruct((nb, 2, nk, tk, w), BF16),
        scratch_shapes=[pltpu.VMEM((half, w), BF16), pltpu.VMEM((half, w), BF16)],
        compiler_params=_params("parallel", "arbitrary"),
        name="fnet_seq_dft",
    )(c_half, sn_half, xc, xs, ogb, rev, rev0)


def _dft_half_matrices(n, tk):
    nk = n // (2 * tk)
    rows = tk + DFT_HALO
    r = DFT_TABLE_SPLIT
    n_a = -(-((nk - 1) * tk + rows) // r)
    s = jnp.arange(n, dtype=jnp.int32)[None, :]
    unit = 2.0 * math.pi / n
    alpha = (((r * jnp.arange(n_a, dtype=jnp.int32)[:, None]) * s) % n).astype(F32) * unit
    beta = ((jnp.arange(r, dtype=jnp.int32)[:, None] * s) % n).astype(F32) * unit
    ca, sa, cb, sb = jnp.cos(alpha), jnp.sin(alpha), jnp.cos(beta), jnp.sin(beta)
    cos = (ca[:, None, :] * cb[None, :, :] - sa[:, None, :] * sb[None, :, :]).reshape(n_a * r, n).astype(BF16)
    nsin = (-(sa[:, None, :] * cb[None, :, :] + ca[:, None, :] * sb[None, :, :])).reshape(n_a * r, n).astype(BF16)
    col, i = jnp.arange(rows)[None, :], jnp.arange(tk)[:, None]
    return cos, nsin, (col == tk - i).astype(BF16), (col == rows - i).astype(BF16)


def _s5_weights(lam_re, lam_im, log_step, b_re, b_im, c_re, c_im, d_skip):
    L = S5_CHUNK
    G, P = lam_re.shape[1:]
    step = jnp.exp(log_step)[..., None]
    lr, li = lam_re, lam_im
    mag = jnp.exp(lr * step)
    ab_re, ab_im = mag * jnp.cos(li * step), mag * jnp.sin(li * step)
    den = lr * lr + li * li
    nr = ab_re - 1.0
    q_re = (nr * lr + ab_im * li) / den
    q_im = (ab_im * lr - nr * li) / den
    bb_re = q_re[..., None] * b_re[None] - q_im[..., None] * b_im[None]
    bb_im = q_re[..., None] * b_im[None] + q_im[..., None] * b_re[None]
    k = jnp.arange(L + 1, dtype=F32)[:, None, None, None]
    pmag = jnp.exp(k * (lr * step))
    pw_re, pw_im = pmag * jnp.cos(k * (li * step)), pmag * jnp.sin(k * (li * step))
    w_re = pw_re[..., None] * bb_re - pw_im[..., None] * bb_im
    w_im = pw_re[..., None] * bb_im + pw_im[..., None] * bb_re
    kern = (jnp.einsum('gop,ktgpc->ktgoc', c_re, w_re[:L]) - jnp.einsum('gop,ktgpc->ktgoc', c_im, w_im[:L]))
    centre = kern[0, 0] + kern[0, 1] + jnp.eye(C_GROUP) * d_skip.reshape(G, 1, C_GROUP)
    lags = jnp.concatenate([kern[1:, 1][::-1], centre[None], kern[1:, 0]], axis=0)
    lags = lags.transpose(1, 3, 0, 2).reshape(G, C_GROUP, (2 * L - 1) * C_GROUP)
    lags = jnp.pad(lags, ((0, 0), (0, 0), (0, C_GROUP)))
    to_rows = lambda a: a.transpose(1, 0, 3, 2).reshape(G, S5_COLS, P)
    p_mat = jnp.concatenate([to_rows(w_re[:L, 0][::-1]), to_rows(w_re[:L, 1]),
                             to_rows(w_im[:L, 0][::-1]), to_rows(w_im[:L, 1])], axis=-1)
    def readout(pr, pi):
        wr = c_re[None] * pr[:, :, None, :] - c_im[None] * pi[:, :, None, :]
        wi = c_re[None] * pi[:, :, None, :] + c_im[None] * pr[:, :, None, :]
        fl = lambda a: a.transpose(1, 3, 0, 2).reshape(G, P, S5_COLS)
        return fl(wr), fl(-wi)
    qf_re, qf_im = readout(pw_re[1:, 0], pw_im[1:, 0])
    qb_re, qb_im = readout(pw_re[1:, 1][::-1], pw_im[1:, 1][::-1])
    z = jnp.zeros_like(qf_re)
    qf = jnp.concatenate([qf_re, z, qf_im, z], axis=1)
    qb = jnp.concatenate([z, qb_re, z, qb_im], axis=1)
    q_mat = jnp.concatenate([qf, qb], axis=1)
    a_re = jnp.concatenate([pw_re[L, 0], pw_re[L, 1]], axis=-1)
    a_im = jnp.concatenate([pw_im[L, 0], pw_im[L, 1]], axis=-1)
    return p_mat.astype(BF16), lags, q_mat.astype(BF16), a_re, a_im


GROUPS_PER_BLOCK = LANES // C_GROUP


def _s5_body(z_ref, p_ref, lag_ref, q_ref, are_ref, aim_ref, o_ref,
             m_scr, v_scr, u_scr, s_scr, ha_scr, hb_scr, y_scr):
    nchunks = z_ref.shape[1] // S5_CHUNK
    rows = GROUPS_PER_BLOCK
    n = nchunks * rows

    @pl.when(pl.program_id(1) == 0)
    def _():
        width = lag_ref.shape[2]
        for g in range(rows):
            lag = lag_ref[g]
            for i in range(S5_CHUNK):
                start = (S5_CHUNK - 1 - i) * C_GROUP
                window = lag if start == 0 else pltpu.roll(lag, width - start, axis=1)
                m_scr[g, i * C_GROUP:(i + 1) * C_GROUP, :] = window[:, :S5_COLS].astype(BF16)

    for i in range(S5_CHUNK):
        t = z_ref[0, pl.ds(i, nchunks, stride=S5_CHUNK), :].T
        for g in range(rows):
            v_scr[g, i * C_GROUP:(i + 1) * C_GROUP, :] = t[g * C_GROUP:(g + 1) * C_GROUP, :]
    for g in range(rows):
        u = v_scr[g].T.astype(BF16)
        u_scr[g] = u
        s = _dot(u, p_ref[g])
        for part in range(2):
            s_scr[part, pl.ds(g, nchunks, stride=rows), :] = s[:, part * LANES:(part + 1) * LANES]

    ar = are_ref[...]
    ai = aim_ref[...]
    is_fwd = lax.broadcasted_iota(jnp.int32, (rows, LANES), 1) < C_STATE
    zero = jnp.zeros((rows, LANES), F32)
    for part in range(2):
        ha_scr[part, 0:rows, :] = zero
        hb_scr[part, n - rows:n, :] = zero

    def step(c, carry):
        hre, him = carry
        r_f = pl.multiple_of(c * rows, rows)
        r_b = pl.multiple_of((nchunks - 1 - c) * rows, rows)
        sre = jnp.where(is_fwd, s_scr[0, pl.ds(r_f, rows), :], s_scr[0, pl.ds(r_b, rows), :])
        sim = jnp.where(is_fwd, s_scr[1, pl.ds(r_f, rows), :], s_scr[1, pl.ds(r_b, rows), :])
        nre = hre * ar - him * ai + sre
        nim = hre * ai + him * ar + sim
        for part, h in enumerate((nre, nim)):
            ha_scr[part, pl.ds(pl.multiple_of(r_f + rows, rows), rows), :] = h
            hb_scr[part, pl.ds(pl.multiple_of(r_b - rows, rows), rows), :] = h
        return nre, nim

    lax.fori_loop(0, nchunks - 1, step, (zero, zero), unroll=15)

    def group_rows(scr, g):
        return jnp.concatenate([scr[part, pl.ds(g, nchunks, stride=rows), :] for part in range(2)],
                               axis=1).astype(BF16)

    for g in range(rows):
        ha = group_rows(ha_scr, g)
        hb = group_rows(hb_scr, g)
        y = (_dot(u_scr[g], m_scr[g]) + _dot(ha, q_ref[g, 0:S5_COLS, :])
             + _dot(hb, q_ref[g, S5_COLS:2 * S5_COLS, :]))
        t = y.T
        for j in range(S5_CHUNK):
            y_scr[j, g * C_GROUP:(g + 1) * C_GROUP, :] = t[j * C_GROUP:(j + 1) * C_GROUP, :]
    for j in range(S5_CHUNK):
        o_ref[0, pl.ds(j, nchunks, stride=S5_CHUNK), :] = y_scr[j].T


def _s5_mixer(zc, nb, seq, weights, layer):
    p_mat, lags, q_mat, a_re, a_im = weights
    width = zc.shape[1]
    nchunks = seq // S5_CHUNK
    gb = GROUPS_PER_BLOCK
    per_block = lambda a: pl.BlockSpec((None, gb) + a.shape[2:], lambda b, n: (layer, b) + (0,) * (a.ndim - 2))
    seq_block = pl.BlockSpec((1, seq, LANES), lambda b, n: (n, 0, b))
    return pl.pallas_call(
        _s5_body,
        grid=(width // LANES, nb),
        in_specs=[seq_block, per_block(p_mat), per_block(lags), per_block(q_mat), per_block(a_re),
                  per_block(a_im)],
        out_specs=seq_block,
        out_shape=jax.ShapeDtypeStruct((nb, seq, width), F32),
        scratch_shapes=[pltpu.VMEM((gb, S5_COLS, S5_COLS), BF16),
                        pltpu.VMEM((gb, S5_COLS, nchunks), F32), pltpu.VMEM((gb, nchunks, S5_COLS), BF16),
                        pltpu.VMEM((2, nchunks * gb, LANES), F32), pltpu.VMEM((2, nchunks * gb, LANES), F32),
                        pltpu.VMEM((2, nchunks * gb, LANES), F32), pltpu.VMEM((S5_CHUNK, LANES, nchunks), F32)],
        compiler_params=_params("parallel", "arbitrary"),
        name="s5_mixer",
    )(zc.reshape(nb, seq, width), p_mat, lags, q_mat, a_re, a_im).reshape(nb * seq, width)


def _out_body(*refs, tile_bounds):
    nx = len(tile_bounds)
    x_refs = refs[:nx]
    ya_ref, yb_ref, yc_ref, wg_ref, bg_ref, ogc_ref, wo_ref, o_ref = refs[nx:]
    acc = _select_rows(x_refs, tile_bounds) + _dot(ya_ref[...], wo_ref[0:A_WIDTH, :])
    acc = acc + _dot(yb_ref[...], wo_ref[A_WIDTH:A_WIDTH + B_WIDTH, :])
    y = _gelu(yc_ref[...].astype(F32))
    yc = y * _sigmoid(_dot(y.astype(BF16), wg_ref[...]) + bg_ref[...])
    yc = _rms(yc, ogc_ref[...]).astype(BF16)
    o_ref[...] = acc + _dot(yc, wo_ref[A_WIDTH + B_WIDTH:, :])


def _out_call(xs, ya, yb, yc, w_glu, b_glu, ogc, w_out, layer):
    t, cw = yc.shape
    d = w_out.shape[2]
    tm = TOKEN_TILE
    x_specs, tile_bounds = _row_tile_specs(xs, tm)
    assert tile_bounds[-1] * tm == t
    row = lambda w: pl.BlockSpec((tm, w), lambda i: (i, 0))
    _, _, nk, tk, bw = yb.shape
    assert tk == tm

    def yb_index(i):
        j = i % (2 * nk)
        return (i // (2 * nk), j // nk, jnp.where(j < nk, j, 2 * nk - 1 - j), 0, 0)

    return pl.pallas_call(
        functools.partial(_out_body, tile_bounds=tile_bounds),
        grid=(t // tm,),
        in_specs=x_specs + [row(A_WIDTH), pl.BlockSpec((None, None, None, tm, bw), yb_index), row(cw),
                            _const_spec((cw, cw), layer),
                            _const_spec((1, cw)), _const_spec((1, cw)), _const_spec(w_out.shape[1:], layer)],
        out_specs=row(d),
        out_shape=jax.ShapeDtypeStruct((t, d), F32),
        compiler_params=_params("parallel"),
        name="glu_out_proj",
    )(*xs, ya, yb, yc, w_glu, b_glu, ogc, w_out)


def _ffn_body(x_ref, wg_ref, wu_ref, wd_ref, gf_ref, o_ref, h_scr, inv_scr, *, final_norm):
    f = pl.program_id(1)

    def ff_tile(h, inv):
        gate = _dot(h, wg_ref[...]) * inv
        a = (gate * _sigmoid(gate) * (_dot(h, wu_ref[...]) * inv)).astype(BF16)
        return _dot(a, wd_ref[...])

    @pl.when(f == 0)
    def _():
        x = x_ref[...]
        h = x.astype(BF16)
        inv = lax.rsqrt(jnp.mean(x * x, axis=-1, keepdims=True) + EPS)
        h_scr[...] = h
        inv_scr[...] = jnp.broadcast_to(inv, inv_scr.shape)
        o_ref[...] = x + ff_tile(h, inv)

    @pl.when(f > 0)
    def _():
        o_ref[...] += ff_tile(h_scr[...], inv_scr[:, 0:1])

    if final_norm:
        @pl.when(f == pl.num_programs(1) - 1)
        def _():
            o_ref[...] = _rms(o_ref[...], gf_ref[...])


def _ffn_call(x, w_gate, w_up, w_down, layer, gf, final_norm, row_start=0, n_rows=None):
    d = x.shape[1]
    ff = w_gate.shape[2]
    n_rows = x.shape[0] if n_rows is None else n_rows
    tm = min(FFN_TOKEN_TILE, n_rows)
    tf = FF_TILE
    assert n_rows % tm == 0 and row_start % tm == 0 and ff % tf == 0
    first = row_start // tm
    return pl.pallas_call(
        functools.partial(_ffn_body, final_norm=final_norm),
        grid=(n_rows // tm, ff // tf),
        in_specs=[pl.BlockSpec((tm, d), lambda i, f: (i + first, 0)),
                  pl.BlockSpec((None, d, tf), lambda i, f: (layer, 0, f)),
                  pl.BlockSpec((None, d, tf), lambda i, f: (layer, 0, f)),
                  pl.BlockSpec((None, tf, d), lambda i, f: (layer, f, 0)), _const_spec((1, d))],
        out_specs=pl.BlockSpec((tm, d), lambda i, f: (i, 0)),
        out_shape=jax.ShapeDtypeStruct((n_rows, d), F32),
        scratch_shapes=[pltpu.VMEM((tm, d), BF16), pltpu.VMEM((tm, LANES), F32)],
        compiler_params=_params("parallel", "arbitrary"),
        name="swiglu_ffn",
    )(x, w_gate, w_up, w_down, gf)


def _dft_matrices(n):
    k = jnp.arange(n, dtype=jnp.int32)
    ang = ((k[:, None] * k[None, :]) % n).astype(F32) * (2.0 * math.pi / n)
    return jnp.cos(ang), jnp.sin(ang)


def kernel(x_prompt, x_sample, norm1_g, w_in, a_v_g, a_ws, a_bs, c_lam_re, c_lam_im, c_log_step, c_b_re, c_b_im,
           c_c_re, c_c_im, c_d, c_w_glu, c_b_glu, out_norm_g, w_out, norm2_g, w_gate, w_up, w_down, final_g):
    seq, d = x_prompt.shape[1:]
    assert x_sample.shape[1:] == (seq, d)
    nb = x_prompt.shape[0] + x_sample.shape[0]
    n_prompt = x_prompt.shape[0] * seq
    depth = w_in.shape[0]
    xs = (x_prompt.reshape(-1, d), x_sample.reshape(-1, d))

    c_half, sn_half, rev, rev0 = _dft_half_matrices(seq, min(DFT_ROW_TILE, seq // 2))
    cds = jnp.concatenate(_dft_matrices(B_HEAD_DIM), axis=1).astype(BF16)
    row = lambda v: v.reshape(1, -1)
    g1, g2 = norm1_g[:, :, None], norm2_g[:, :, None]
    w_in, w_out, c_w_glu, w_gate, w_up, w_down = (
        w.astype(BF16) for w in (w_in * g1, w_out, c_w_glu, w_gate * g2, w_up * g2, w_down))
    s5w = jax.vmap(_s5_weights)(c_lam_re, c_lam_im, c_log_step, c_b_re, c_b_im, c_c_re, c_c_im, c_d)

    ws = a_ws.astype(BF16)
    bs = jnp.broadcast_to(a_bs[..., None], a_bs.shape + (A_HEAD_DIM,))

    for l in range(depth):
        og = out_norm_g[l]
        ya, xc, xsin, zc = _in_call(xs, w_in, l, row(a_v_g[l]), ws, bs, row(og[:A_WIDTH]), cds)
        yb = _dft_call(c_half, sn_half, rev, rev0, xc, xsin, row(og[A_WIDTH:A_WIDTH + B_WIDTH]), seq)
        yc = _s5_mixer(zc, nb, seq, s5w, l)
        x = _out_call(xs, ya, yb, yc, c_w_glu, row(c_b_glu[l]), row(og[A_WIDTH + B_WIDTH:]), w_out, l)
        ffn = functools.partial(_ffn_call, x, w_gate, w_up, w_down, l, row(final_g))
        if l < depth - 1:
            xs = (ffn(final_norm=False),)
        else:
            y_prompt = ffn(final_norm=True, row_start=0, n_rows=n_prompt)
            y_sample = ffn(final_norm=True, row_start=n_prompt, n_rows=x.shape[0] - n_prompt)

    return (y_prompt.reshape(x_prompt.shape), y_sample.reshape(x_sample.shape))
```

```python
import functools
import math

import jax
import jax.numpy as jnp
from jax import lax
from jax.experimental import pallas as pl
from jax.experimental.pallas import tpu as pltpu

F32 = jnp.float32
BF16 = jnp.bfloat16

EPS = 1e-6
A_HEADS = 6
A_HEAD_DIM = 128
A_WIDTH = A_HEADS * A_HEAD_DIM
CHUNK = 128
B_HEADS = 4
B_HEAD_DIM = 128
B_WIDTH = B_HEADS * B_HEAD_DIM
C_GROUP = 16
C_STATE = 64

LANES = 128
V7X_MXU_WIDTH = 256
V7X_VMEM_BYTES = 64 * 1024 * 1024
VMEM_LIMIT_BYTES = V7X_VMEM_BYTES - 8 * 1024 * 1024
S5_CHUNK = V7X_MXU_WIDTH // C_GROUP
S5_COLS = S5_CHUNK * C_GROUP
DFT_TABLE_SPLIT = 64

TOKEN_TILE = 512
FFN_TOKEN_TILE = 1024
FF_TILE = 512
DFT_ROW_TILE = 512
GELU_C = math.sqrt(2.0 / math.pi)


def _gelu(x):
    return 0.5 * x * (1.0 + jnp.tanh(GELU_C * (x + 0.044715 * (x * x * x))))


def _sigmoid(x):
    return 1.0 / (1.0 + jnp.exp(-x))


def _rms(x, g):
    return x * lax.rsqrt(jnp.mean(x * x, axis=-1, keepdims=True) + EPS) * g


def _dot(a, b):
    return jnp.dot(a, b, preferred_element_type=F32)


def _const_spec(shape, layer=None):
    nd = len(shape)
    if layer is None:
        return pl.BlockSpec(shape, lambda *_: (0,) * nd, pipeline_mode=pl.Buffered(1))
    return pl.BlockSpec((None,) + tuple(shape), lambda *_: (layer,) + (0,) * nd, pipeline_mode=pl.Buffered(1))


def _row_tile_specs(xs, tm):
    specs, bounds, start = [], [], 0
    for x in xs:
        n = x.shape[0] // tm
        assert n * tm == x.shape[0] and tm % CHUNK == 0, (x.shape, tm)
        specs.append(pl.BlockSpec((tm, x.shape[1]),
                                  lambda i, *_, start=start, n=n: (jnp.clip(i - start, 0, n - 1), 0)))
        start += n
        bounds.append(start)
    return specs, tuple(bounds)


def _select_rows(x_refs, bounds):
    x = x_refs[-1][...]
    for k in range(len(x_refs) - 2, -1, -1):
        x = jnp.where(pl.program_id(0) < bounds[k], x_refs[k][...], x)
    return x


def _params(*sem):
    return pltpu.CompilerParams(dimension_semantics=sem, vmem_limit_bytes=VMEM_LIMIT_BYTES)


def _in_body(*refs, tile_bounds):
    nx = len(tile_bounds)
    x_refs = refs[:nx]
    (w_ref, vg_ref, ws_ref, bs_ref, oga_ref, cds_ref,
     ya_ref, xc_ref, xs_ref, zc_ref, z0_scr, z1_scr, ya_scr) = refs[nx:]
    tm = ya_ref.shape[0]
    nch = tm // CHUNK
    i = pl.program_id(0)

    def step(z_new, z_old):
        x = _select_rows(x_refs, tile_bounds)
        h = (x * lax.rsqrt(jnp.mean(x * x, axis=-1, keepdims=True) + EPS)).astype(BF16)
        zb = _dot(h, w_ref[:, 2 * A_WIDTH:2 * A_WIDTH + B_WIDTH]).astype(BF16)
        slab = 2 * A_WIDTH // A_HEADS
        for hd in range(A_HEADS):
            z_new[:, hd * slab:(hd + 1) * slab] = _dot(h, w_ref[:, hd * slab:(hd + 1) * slab])
            lo = hd * A_HEAD_DIM
            u = _gelu(z_old[:, lo:lo + A_HEAD_DIM])
            v = _gelu(z_old[:, A_WIDTH + lo:A_WIDTH + lo + A_HEAD_DIM])
            v = _rms(v, vg_ref[:, lo:lo + A_HEAD_DIM]).astype(BF16)
            v_wide = jnp.concatenate([v[c * CHUNK:(c + 1) * CHUNK, :] for c in range(nch)], axis=1)
            mixed = _dot(ws_ref[hd], v_wide)
            for c in range(nch):
                r = c * CHUNK
                ya_scr[r:r + CHUNK, lo:lo + A_HEAD_DIM] = u[r:r + CHUNK, :] * (
                    mixed[:, c * A_HEAD_DIM:(c + 1) * A_HEAD_DIM] + bs_ref[hd])
        zc_ref[...] = _dot(h, w_ref[:, 2 * A_WIDTH + B_WIDTH:])
        for g in range(B_HEADS):
            lo = g * B_HEAD_DIM
            t = _dot(zb[:, lo:lo + B_HEAD_DIM], cds_ref[...])
            xc_ref[:, lo:lo + B_HEAD_DIM] = t[:, :B_HEAD_DIM].astype(BF16)
            xs_ref[:, lo:lo + B_HEAD_DIM] = t[:, B_HEAD_DIM:].astype(BF16)
        ya_ref[...] = _rms(ya_scr[...], oga_ref[...]).astype(BF16)

    @pl.when(i == 0)
    def _():
        z1_scr[...] = jnp.zeros(z1_scr.shape, F32)

    @pl.when(i % 2 == 0)
    def _():
        step(z0_scr, z1_scr)

    @pl.when(i % 2 == 1)
    def _():
        step(z1_scr, z0_scr)


def _in_call(xs, w_in, layer, vg, ws, bs, oga, cds):
    d = xs[0].shape[1]
    n_in = w_in.shape[2]
    c_width = n_in - 2 * A_WIDTH - B_WIDTH
    tm = TOKEN_TILE
    x_specs, tile_bounds = _row_tile_specs(xs, tm)
    nt = tile_bounds[-1]
    t = nt * tm
    this_tile = lambda w: pl.BlockSpec((tm, w), lambda i: (jnp.minimum(i, nt - 1), 0))
    prev_tile = lambda w: pl.BlockSpec((tm, w), lambda i: (jnp.maximum(i - 1, 0), 0))
    return pl.pallas_call(
        functools.partial(_in_body, tile_bounds=tile_bounds),
        grid=(nt + 1,),
        in_specs=x_specs + [_const_spec((d, n_in), layer), _const_spec((1, A_WIDTH)),
                            _const_spec(ws.shape[1:], layer), _const_spec(bs.shape[1:], layer),
                            _const_spec((1, A_WIDTH)),
                            _const_spec(cds.shape)],
        out_specs=[prev_tile(A_WIDTH), this_tile(B_WIDTH), this_tile(B_WIDTH), this_tile(c_width)],
        out_shape=[jax.ShapeDtypeStruct((t, A_WIDTH), BF16), jax.ShapeDtypeStruct((t, B_WIDTH), BF16),
                   jax.ShapeDtypeStruct((t, B_WIDTH), BF16), jax.ShapeDtypeStruct((t, c_width), F32)],
        scratch_shapes=[pltpu.VMEM((tm, 2 * A_WIDTH), F32), pltpu.VMEM((tm, 2 * A_WIDTH), F32),
                        pltpu.VMEM((tm, A_WIDTH), F32)],
        compiler_params=_params("arbitrary"),
        name="in_proj_gmlp",
    )(*xs, w_in, vg, ws, bs, oga, cds)


DFT_HALO = 16


def _dft_body(c_ref, sn_ref, xc_ref, xs_ref, og_ref, rev_ref, rev0_ref, o_ref, xcf_scr, xsf_scr, *, scale):
    tk = o_ref.shape[1]
    rows = c_ref.shape[0]
    half = xcf_scr.shape[0]
    n = xc_ref.shape[0]

    @pl.when(pl.program_id(1) == 0)
    def _():
        for m in range(half // tk):
            lo = m * tk
            start, sel = (n - rows, rev0_ref) if m == 0 else (n - lo - tk, rev_ref)
            pc = _dot(sel[...], xc_ref[start:start + rows, :])
            ps = _dot(sel[...], xs_ref[start:start + rows, :])
            xcf_scr[lo:lo + tk, :] = (xc_ref[lo:lo + tk, :].astype(F32) + pc).astype(BF16)
            xsf_scr[lo:lo + tk, :] = (xs_ref[lo:lo + tk, :].astype(F32) - ps).astype(BF16)

    sign = 1.0 - 2.0 * (lax.broadcasted_iota(jnp.int32, (rows, 1), 0) % 2).astype(F32)
    a = _dot(c_ref[...], xcf_scr[...]) + sign * xc_ref[half:half + 1, :].astype(F32)
    bn = _dot(sn_ref[...], xsf_scr[...])
    o_ref[0] = _rms((a + bn)[:tk] * scale, og_ref[...]).astype(BF16)
    mirrored = _rms((a - bn) * scale, og_ref[...]).astype(BF16)
    o_ref[1] = _dot(rev_ref[...], mirrored).astype(BF16)


def _dft_call(c_half, sn_half, rev, rev0, xc, xs, ogb, seq):
    t, w = xc.shape
    nb = t // seq
    tk, rows = rev.shape
    half = seq // 2
    nk = half // tk
    assert nk * tk == half and tk % 2 == 0 and c_half.shape[0] >= (nk - 1) * tk + rows
    scale = 1.0 / math.sqrt(seq * B_HEAD_DIM)
    lhs = pl.BlockSpec((pl.Element(rows), pl.Element(half)), lambda b, k: (k * tk, 0))
    rhs = pl.BlockSpec((seq, w), lambda b, k: (b, 0))
    return pl.pallas_call(
        functools.partial(_dft_body, scale=scale),
        grid=(nb, nk),
        in_specs=[lhs, lhs, rhs, rhs, pl.BlockSpec((1, w), lambda b, k: (0, 0)), _const_spec(rev.shape),
                  _const_spec(rev0.shape)],
        out_specs=pl.BlockSpec((None, 2, None, tk, w), lambda b, k: (b, 0, k, 0, 0)),
        out_shape=jax.ShapeDtypeStruct((nb, 2, nk, tk, w), BF16),
        scratch_shapes=[pltpu.VMEM((half, w), BF16), pltpu.VMEM((half, w), BF16)],
        compiler_params=_params("parallel", "arbitrary"),
        name="fnet_seq_dft",
    )(c_half, sn_half, xc, xs, ogb, rev, rev0)


def _dft_half_matrices(n, tk):
    nk = n // (2 * tk)
    rows = tk + DFT_HALO
    r = DFT_TABLE_SPLIT
    n_a = -(-((nk - 1) * tk + rows) // r)
    s = jnp.arange(n, dtype=jnp.int32)[None, :]
    unit = 2.0 * math.pi / n
    alpha = (((r * jnp.arange(n_a, dtype=jnp.int32)[:, None]) * s) % n).astype(F32) * unit
    beta = ((jnp.arange(r, dtype=jnp.int32)[:, None] * s) % n).astype(F32) * unit
    ca, sa, cb, sb = jnp.cos(alpha), jnp.sin(alpha), jnp.cos(beta), jnp.sin(beta)
    cos = (ca[:, None, :] * cb[None, :, :] - sa[:, None, :] * sb[None, :, :]).reshape(n_a * r, n).astype(BF16)
    nsin = (-(sa[:, None, :] * cb[None, :, :] + ca[:, None, :] * sb[None, :, :])).reshape(n_a * r, n).astype(BF16)
    col, i = jnp.arange(rows)[None, :], jnp.arange(tk)[:, None]
    return cos, nsin, (col == tk - i).astype(BF16), (col == rows - i).astype(BF16)


def _s5_weights(lam_re, lam_im, log_step, b_re, b_im, c_re, c_im, d_skip):
    L = S5_CHUNK
    G, P = lam_re.shape[1:]
    step = jnp.exp(log_step)[..., None]
    lr, li = lam_re, lam_im
    mag = jnp.exp(lr * step)
    ab_re, ab_im = mag * jnp.cos(li * step), mag * jnp.sin(li * step)
    den = lr * lr + li * li
    nr = ab_re - 1.0
    q_re = (nr * lr + ab_im * li) / den
    q_im = (ab_im * lr - nr * li) / den
    bb_re = q_re[..., None] * b_re[None] - q_im[..., None] * b_im[None]
    bb_im = q_re[..., None] * b_im[None] + q_im[..., None] * b_re[None]
    k = jnp.arange(L + 1, dtype=F32)[:, None, None, None]
    pmag = jnp.exp(k * (lr * step))
    pw_re, pw_im = pmag * jnp.cos(k * (li * step)), pmag * jnp.sin(k * (li * step))
    w_re = pw_re[..., None] * bb_re - pw_im[..., None] * bb_im
    w_im = pw_re[..., None] * bb_im + pw_im[..., None] * bb_re
    kern = (jnp.einsum('gop,ktgpc->ktgoc', c_re, w_re[:L]) - jnp.einsum('gop,ktgpc->ktgoc', c_im, w_im[:L]))
    centre = kern[0, 0] + kern[0, 1] + jnp.eye(C_GROUP) * d_skip.reshape(G, 1, C_GROUP)
    lags = jnp.concatenate([kern[1:, 1][::-1], centre[None], kern[1:, 0]], axis=0)
    lags = lags.transpose(1, 3, 0, 2).reshape(G, C_GROUP, (2 * L - 1) * C_GROUP)
    lags = jnp.pad(lags, ((0, 0), (0, 0), (0, C_GROUP)))
    to_rows = lambda a: a.transpose(1, 0, 3, 2).reshape(G, S5_COLS, P)
    p_mat = jnp.concatenate([to_rows(w_re[:L, 0][::-1]), to_rows(w_re[:L, 1]),
                             to_rows(w_im[:L, 0][::-1]), to_rows(w_im[:L, 1])], axis=-1)
    def readout(pr, pi):
        wr = c_re[None] * pr[:, :, None, :] - c_im[None] * pi[:, :, None, :]
        wi = c_re[None] * pi[:, :, None, :] + c_im[None] * pr[:, :, None, :]
        fl = lambda a: a.transpose(1, 3, 0, 2).reshape(G, P, S5_COLS)
        return fl(wr), fl(-wi)
    qf_re, qf_im = readout(pw_re[1:, 0], pw_im[1:, 0])
    qb_re, qb_im = readout(pw_re[1:, 1][::-1], pw_im[1:, 1][::-1])
    z = jnp.zeros_like(qf_re)
    qf = jnp.concatenate([qf_re, z, qf_im, z], axis=1)
    qb = jnp.concatenate([z, qb_re, z, qb_im], axis=1)
    q_mat = jnp.concatenate([qf, qb], axis=1)
    a_re = jnp.concatenate([pw_re[L, 0], pw_re[L, 1]], axis=-1)
    a_im = jnp.concatenate([pw_im[L, 0], pw_im[L, 1]], axis=-1)
    return p_mat.astype(BF16), lags, q_mat.astype(BF16), a_re, a_im


GROUPS_PER_BLOCK = LANES // C_GROUP


def _s5_body(z_ref, p_ref, lag_ref, q_ref, are_ref, aim_ref, o_ref,
             m_scr, v_scr, u_scr, s_scr, ha_scr, hb_scr, y_scr):
    nchunks = z_ref.shape[1] // S5_CHUNK
    rows = GROUPS_PER_BLOCK
    n = nchunks * rows

    @pl.when(pl.program_id(1) == 0)
    def _():
        width = lag_ref.shape[2]
        for g in range(rows):
            lag = lag_ref[g]
            for i in range(S5_CHUNK):
                start = (S5_CHUNK - 1 - i) * C_GROUP
                window = lag if start == 0 else pltpu.roll(lag, width - start, axis=1)
                m_scr[g, i * C_GROUP:(i + 1) * C_GROUP, :] = window[:, :S5_COLS].astype(BF16)

    for i in range(S5_CHUNK):
        t = z_ref[0, pl.ds(i, nchunks, stride=S5_CHUNK), :].T
        for g in range(rows):
            v_scr[g, i * C_GROUP:(i + 1) * C_GROUP, :] = t[g * C_GROUP:(g + 1) * C_GROUP, :]
    for g in range(rows):
        u = v_scr[g].T.astype(BF16)
        u_scr[g] = u
        s = _dot(u, p_ref[g])
        for part in range(2):
            s_scr[part, pl.ds(g, nchunks, stride=rows), :] = s[:, part * LANES:(part + 1) * LANES]

    ar = are_ref[...]
    ai = aim_ref[...]
    is_fwd = lax.broadcasted_iota(jnp.int32, (rows, LANES), 1) < C_STATE
    zero = jnp.zeros((rows, LANES), F32)
    for part in range(2):
        ha_scr[part, 0:rows, :] = zero
        hb_scr[part, n - rows:n, :] = zero

    def tile_row(index):
        return index * rows if isinstance(index, int) else pl.multiple_of(index * rows, rows)

    def local_state(c):
        r_f = tile_row(c)
        r_b = tile_row(nchunks - 1 - c)
        return (jnp.where(is_fwd, s_scr[0, pl.ds(r_f, rows), :], s_scr[0, pl.ds(r_b, rows), :]),
                jnp.where(is_fwd, s_scr[1, pl.ds(r_f, rows), :], s_scr[1, pl.ds(r_b, rows), :]))

    def store_state(c, hre, him):
        r_f = tile_row(c + 1)
        r_b = tile_row(nchunks - 2 - c)
        for part, h in enumerate((hre, him)):
            ha_scr[part, pl.ds(r_f, rows), :] = h
            hb_scr[part, pl.ds(r_b, rows), :] = h

    def advance(hre, him, cre, cim, sre, sim):
        return hre * cre - him * cim + sre, hre * cim + him * cre + sim

    ar2 = ar * ar - ai * ai
    ai2 = 2.0 * ar * ai

    def pair(p, carry):
        hre, him = carry
        c = 2 * p
        s0re, s0im = local_state(c)
        s1re, s1im = local_state(c + 1)
        store_state(c, *advance(hre, him, ar, ai, s0re, s0im))
        tre, tim = advance(s0re, s0im, ar, ai, s1re, s1im)
        nre, nim = advance(hre, him, ar2, ai2, tre, tim)
        store_state(c + 1, nre, nim)
        return nre, nim

    steps = nchunks - 1
    hre, him = lax.fori_loop(0, steps // 2, pair, (zero, zero), unroll=8)
    if steps % 2:
        store_state(steps - 1, *advance(hre, him, ar, ai, *local_state(steps - 1)))

    def group_rows(scr, g):
        return jnp.concatenate([scr[part, pl.ds(g, nchunks, stride=rows), :] for part in range(2)],
                               axis=1).astype(BF16)

    for g in range(rows):
        ha = group_rows(ha_scr, g)
        hb = group_rows(hb_scr, g)
        y = (_dot(u_scr[g], m_scr[g]) + _dot(ha, q_ref[g, 0:S5_COLS, :])
             + _dot(hb, q_ref[g, S5_COLS:2 * S5_COLS, :]))
        t = y.T
        for j in range(S5_CHUNK):
            y_scr[j, g * C_GROUP:(g + 1) * C_GROUP, :] = t[j * C_GROUP:(j + 1) * C_GROUP, :]
    for j in range(S5_CHUNK):
        o_ref[0, pl.ds(j, nchunks, stride=S5_CHUNK), :] = y_scr[j].T


def _s5_mixer(zc, nb, seq, weights, layer):
    p_mat, lags, q_mat, a_re, a_im = weights
    width = zc.shape[1]
    nchunks = seq // S5_CHUNK
    gb = GROUPS_PER_BLOCK
    per_block = lambda a: pl.BlockSpec((None, gb) + a.shape[2:], lambda b, n: (layer, b) + (0,) * (a.ndim - 2))
    seq_block = pl.BlockSpec((1, seq, LANES), lambda b, n: (n, 0, b))
    return pl.pallas_call(
        _s5_body,
        grid=(width // LANES, nb),
        in_specs=[seq_block, per_block(p_mat), per_block(lags), per_block(q_mat), per_block(a_re),
                  per_block(a_im)],
        out_specs=seq_block,
        out_shape=jax.ShapeDtypeStruct((nb, seq, width), F32),
        scratch_shapes=[pltpu.VMEM((gb, S5_COLS, S5_COLS), BF16),
                        pltpu.VMEM((gb, S5_COLS, nchunks), F32), pltpu.VMEM((gb, nchunks, S5_COLS), BF16),
                        pltpu.VMEM((2, nchunks * gb, LANES), F32), pltpu.VMEM((2, nchunks * gb, LANES), F32),
                        pltpu.VMEM((2, nchunks * gb, LANES), F32), pltpu.VMEM((S5_CHUNK, LANES, nchunks), F32)],
        compiler_params=_params("parallel", "arbitrary"),
        name="s5_mixer",
    )(zc.reshape(nb, seq, width), p_mat, lags, q_mat, a_re, a_im).reshape(nb * seq, width)


def _out_body(*refs, tile_bounds):
    nx = len(tile_bounds)
    x_refs = refs[:nx]
    ya_ref, yb_ref, yc_ref, wg_ref, bg_ref, ogc_ref, wo_ref, o_ref = refs[nx:]
    acc = _select_rows(x_refs, tile_bounds) + _dot(ya_ref[...], wo_ref[0:A_WIDTH, :])
    acc = acc + _dot(yb_ref[...], wo_ref[A_WIDTH:A_WIDTH + B_WIDTH, :])
    y = _gelu(yc_ref[...].astype(F32))
    yc = y * _sigmoid(_dot(y.astype(BF16), wg_ref[...]) + bg_ref[...])
    yc = _rms(yc, ogc_ref[...]).astype(BF16)
    o_ref[...] = acc + _dot(yc, wo_ref[A_WIDTH + B_WIDTH:, :])


def _out_call(xs, ya, yb, yc, w_glu, b_glu, ogc, w_out, layer):
    t, cw = yc.shape
    d = w_out.shape[2]
    tm = TOKEN_TILE
    x_specs, tile_bounds = _row_tile_specs(xs, tm)
    assert tile_bounds[-1] * tm == t
    row = lambda w: pl.BlockSpec((tm, w), lambda i: (i, 0))
    _, _, nk, tk, bw = yb.shape
    assert tk == tm

    def yb_index(i):
        j = i % (2 * nk)
        return (i // (2 * nk), j // nk, jnp.where(j < nk, j, 2 * nk - 1 - j), 0, 0)

    return pl.pallas_call(
        functools.partial(_out_body, tile_bounds=tile_bounds),
        grid=(t // tm,),
        in_specs=x_specs + [row(A_WIDTH), pl.BlockSpec((None, None, None, tm, bw), yb_index), row(cw),
                            _const_spec((cw, cw), layer),
                            _const_spec((1, cw)), _const_spec((1, cw)), _const_spec(w_out.shape[1:], layer)],
        out_specs=row(d),
        out_shape=jax.ShapeDtypeStruct((t, d), F32),
        compiler_params=_params("parallel"),
        name="glu_out_proj",
    )(*xs, ya, yb, yc, w_glu, b_glu, ogc, w_out)


def _ffn_body(x_ref, wg_ref, wu_ref, wd_ref, gf_ref, o_ref, h_scr, inv_scr, *, final_norm):
    f = pl.program_id(1)

    def ff_tile(h, inv):
        gate = _dot(h, wg_ref[...]) * inv
        a = (gate * _sigmoid(gate) * (_dot(h, wu_ref[...]) * inv)).astype(BF16)
        return _dot(a, wd_ref[...])

    @pl.when(f == 0)
    def _():
        x = x_ref[...]
        h = x.astype(BF16)
        inv = lax.rsqrt(jnp.mean(x * x, axis=-1, keepdims=True) + EPS)
        h_scr[...] = h
        inv_scr[...] = jnp.broadcast_to(inv, inv_scr.shape)
        o_ref[...] = x + ff_tile(h, inv)

    @pl.when(f > 0)
    def _():
        o_ref[...] += ff_tile(h_scr[...], inv_scr[:, 0:1])

    if final_norm:
        @pl.when(f == pl.num_programs(1) - 1)
        def _():
            o_ref[...] = _rms(o_ref[...], gf_ref[...])


def _ffn_call(x, w_gate, w_up, w_down, layer, gf, final_norm, row_start=0, n_rows=None):
    d = x.shape[1]
    ff = w_gate.shape[2]
    n_rows = x.shape[0] if n_rows is None else n_rows
    tm = min(FFN_TOKEN_TILE, n_rows)
    tf = FF_TILE
    assert n_rows % tm == 0 and row_start % tm == 0 and ff % tf == 0
    first = row_start // tm
    return pl.pallas_call(
        functools.partial(_ffn_body, final_norm=final_norm),
        grid=(n_rows // tm, ff // tf),
        in_specs=[pl.BlockSpec((tm, d), lambda i, f: (i + first, 0)),
                  pl.BlockSpec((None, d, tf), lambda i, f: (layer, 0, f)),
                  pl.BlockSpec((None, d, tf), lambda i, f: (layer, 0, f)),
                  pl.BlockSpec((None, tf, d), lambda i, f: (layer, f, 0)), _const_spec((1, d))],
        out_specs=pl.BlockSpec((tm, d), lambda i, f: (i, 0)),
        out_shape=jax.ShapeDtypeStruct((n_rows, d), F32),
        scratch_shapes=[pltpu.VMEM((tm, d), BF16), pltpu.VMEM((tm, LANES), F32)],
        compiler_params=_params("parallel", "arbitrary"),
        name="swiglu_ffn",
    )(x, w_gate, w_up, w_down, gf)


def _dft_matrices(n):
    k = jnp.arange(n, dtype=jnp.int32)
    ang = ((k[:, None] * k[None, :]) % n).astype(F32) * (2.0 * math.pi / n)
    return jnp.cos(ang), jnp.sin(ang)


def kernel(x_prompt, x_sample, norm1_g, w_in, a_v_g, a_ws, a_bs, c_lam_re, c_lam_im, c_log_step, c_b_re, c_b_im,
           c_c_re, c_c_im, c_d, c_w_glu, c_b_glu, out_norm_g, w_out, norm2_g, w_gate, w_up, w_down, final_g):
    seq, d = x_prompt.shape[1:]
    assert x_sample.shape[1:] == (seq, d)
    nb = x_prompt.shape[0] + x_sample.shape[0]
    n_prompt = x_prompt.shape[0] * seq
    depth = w_in.shape[0]
    xs = (x_prompt.reshape(-1, d), x_sample.reshape(-1, d))

    c_half, sn_half, rev, rev0 = _dft_half_matrices(seq, min(DFT_ROW_TILE, seq // 2))
    cds = jnp.concatenate(_dft_matrices(B_HEAD_DIM), axis=1).astype(BF16)
    row = lambda v: v.reshape(1, -1)
    g1, g2 = norm1_g[:, :, None], norm2_g[:, :, None]
    w_in, w_out, c_w_glu, w_gate, w_up, w_down = (
        w.astype(BF16) for w in (w_in * g1, w_out, c_w_glu, w_gate * g2, w_up * g2, w_down))
    s5w = jax.vmap(_s5_weights)(c_lam_re, c_lam_im, c_log_step, c_b_re, c_b_im, c_c_re, c_c_im, c_d)

    ws = a_ws.astype(BF16)
    bs = jnp.broadcast_to(a_bs[..., None], a_bs.shape + (A_HEAD_DIM,))

    for l in range(depth):
        og = out_norm_g[l]
        ya, xc, xsin, zc = _in_call(xs, w_in, l, row(a_v_g[l]), ws, bs, row(og[:A_WIDTH]), cds)
        yb = _dft_call(c_half, sn_half, rev, rev0, xc, xsin, row(og[A_WIDTH:A_WIDTH + B_WIDTH]), seq)
        yc = _s5_mixer(zc, nb, seq, s5w, l)
        x = _out_call(xs, ya, yb, yc, c_w_glu, row(c_b_glu[l]), row(og[A_WIDTH + B_WIDTH:]), w_out, l)
        ffn = functools.partial(_ffn_call, x, w_gate, w_up, w_down, l, row(final_g))
        if l < depth - 1:
            xs = (ffn(final_norm=False),)
        else:
            y_prompt = ffn(final_norm=True, row_start=0, n_rows=n_prompt)
            y_sample = ffn(final_norm=True, row_start=n_prompt, n_rows=x.shape[0] - n_prompt)

    return (y_prompt.reshape(x_prompt.shape), y_sample.reshape(x_sample.shape))
```

```python
import functools
import math

import jax
import jax.numpy as jnp
from jax import lax
from jax.experimental import pallas as pl
from jax.experimental.pallas import tpu as pltpu

F32 = jnp.float32
BF16 = jnp.bfloat16

EPS = 1e-6
A_HEADS = 6
A_HEAD_DIM = 128
A_WIDTH = A_HEADS * A_HEAD_DIM
CHUNK = 128
B_HEADS = 4
B_HEAD_DIM = 128
B_WIDTH = B_HEADS * B_HEAD_DIM
C_GROUP = 16
C_STATE = 64

LANES = 128
V7X_MXU_WIDTH = 256
V7X_VMEM_BYTES = 64 * 1024 * 1024
VMEM_LIMIT_BYTES = V7X_VMEM_BYTES - 8 * 1024 * 1024
S5_CHUNK = V7X_MXU_WIDTH // C_GROUP
S5_COLS = S5_CHUNK * C_GROUP
DFT_TABLE_SPLIT = 64

TOKEN_TILE = 512
FFN_TOKEN_TILE = 1024
FF_TILE = 512
DFT_ROW_TILE = 512
GELU_C = math.sqrt(2.0 / math.pi)


def _gelu(x):
    return 0.5 * x * (1.0 + jnp.tanh(GELU_C * (x + 0.044715 * (x * x * x))))


def _sigmoid(x):
    return 1.0 / (1.0 + jnp.exp(-x))


def _rms(x, g):
    return x * lax.rsqrt(jnp.mean(x * x, axis=-1, keepdims=True) + EPS) * g


def _dot(a, b):
    return jnp.dot(a, b, preferred_element_type=F32)


def _const_spec(shape, layer=None):
    nd = len(shape)
    if layer is None:
        return pl.BlockSpec(shape, lambda *_: (0,) * nd, pipeline_mode=pl.Buffered(1))
    return pl.BlockSpec((None,) + tuple(shape), lambda *_: (layer,) + (0,) * nd, pipeline_mode=pl.Buffered(1))


def _row_tile_specs(xs, tm):
    specs, bounds, start = [], [], 0
    for x in xs:
        n = x.shape[0] // tm
        assert n * tm == x.shape[0] and tm % CHUNK == 0, (x.shape, tm)
        specs.append(pl.BlockSpec((tm, x.shape[1]),
                                  lambda i, *_, start=start, n=n: (jnp.clip(i - start, 0, n - 1), 0)))
        start += n
        bounds.append(start)
    return specs, tuple(bounds)


def _select_rows(x_refs, bounds):
    x = x_refs[-1][...]
    for k in range(len(x_refs) - 2, -1, -1):
        x = jnp.where(pl.program_id(0) < bounds[k], x_refs[k][...], x)
    return x


def _params(*sem):
    return pltpu.CompilerParams(dimension_semantics=sem, vmem_limit_bytes=VMEM_LIMIT_BYTES)


def _in_body(*refs, tile_bounds):
    nx = len(tile_bounds)
    x_refs = refs[:nx]
    (w_ref, vg_ref, ws_ref, bs_ref, oga_ref, cds_ref,
     ya_ref, xc_ref, xs_ref, zc_ref, z0_scr, z1_scr, ya_scr) = refs[nx:]
    tm = ya_ref.shape[0]
    nch = tm // CHUNK
    i = pl.program_id(0)

    def step(z_new, z_old):
        x = _select_rows(x_refs, tile_bounds)
        h = (x * lax.rsqrt(jnp.mean(x * x, axis=-1, keepdims=True) + EPS)).astype(BF16)
        zb = _dot(h, w_ref[:, 2 * A_WIDTH:2 * A_WIDTH + B_WIDTH]).astype(BF16)
        slab = 2 * A_WIDTH // A_HEADS
        for hd in range(A_HEADS):
            z_new[:, hd * slab:(hd + 1) * slab] = _dot(h, w_ref[:, hd * slab:(hd + 1) * slab])
            lo = hd * A_HEAD_DIM
            u = _gelu(z_old[:, lo:lo + A_HEAD_DIM])
            v = _gelu(z_old[:, A_WIDTH + lo:A_WIDTH + lo + A_HEAD_DIM])
            v = _rms(v, vg_ref[:, lo:lo + A_HEAD_DIM]).astype(BF16)
            v_wide = jnp.concatenate([v[c * CHUNK:(c + 1) * CHUNK, :] for c in range(nch)], axis=1)
            mixed = _dot(ws_ref[hd], v_wide)
            for c in range(nch):
                r = c * CHUNK
                ya_scr[r:r + CHUNK, lo:lo + A_HEAD_DIM] = u[r:r + CHUNK, :] * (
                    mixed[:, c * A_HEAD_DIM:(c + 1) * A_HEAD_DIM] + bs_ref[hd])
        zc_ref[...] = _dot(h, w_ref[:, 2 * A_WIDTH + B_WIDTH:])
        for g in range(B_HEADS):
            lo = g * B_HEAD_DIM
            t = _dot(zb[:, lo:lo + B_HEAD_DIM], cds_ref[...])
            xc_ref[:, lo:lo + B_HEAD_DIM] = t[:, :B_HEAD_DIM].astype(BF16)
            xs_ref[:, lo:lo + B_HEAD_DIM] = t[:, B_HEAD_DIM:].astype(BF16)
        ya_ref[...] = _rms(ya_scr[...], oga_ref[...]).astype(BF16)

    @pl.when(i == 0)
    def _():
        z1_scr[...] = jnp.zeros(z1_scr.shape, F32)

    @pl.when(i % 2 == 0)
    def _():
        step(z0_scr, z1_scr)

    @pl.when(i % 2 == 1)
    def _():
        step(z1_scr, z0_scr)


def _in_call(xs, w_in, layer, vg, ws, bs, oga, cds):
    d = xs[0].shape[1]
    n_in = w_in.shape[2]
    c_width = n_in - 2 * A_WIDTH - B_WIDTH
    tm = TOKEN_TILE
    x_specs, tile_bounds = _row_tile_specs(xs, tm)
    nt = tile_bounds[-1]
    t = nt * tm
    this_tile = lambda w: pl.BlockSpec((tm, w), lambda i: (jnp.minimum(i, nt - 1), 0))
    prev_tile = lambda w: pl.BlockSpec((tm, w), lambda i: (jnp.maximum(i - 1, 0), 0))
    return pl.pallas_call(
        functools.partial(_in_body, tile_bounds=tile_bounds),
        grid=(nt + 1,),
        in_specs=x_specs + [_const_spec((d, n_in), layer), _const_spec((1, A_WIDTH)),
                            _const_spec(ws.shape[1:], layer), _const_spec(bs.shape[1:], layer),
                            _const_spec((1, A_WIDTH)),
                            _const_spec(cds.shape)],
        out_specs=[prev_tile(A_WIDTH), this_tile(B_WIDTH), this_tile(B_WIDTH), this_tile(c_width)],
        out_shape=[jax.ShapeDtypeStruct((t, A_WIDTH), BF16), jax.ShapeDtypeStruct((t, B_WIDTH), BF16),
                   jax.ShapeDtypeStruct((t, B_WIDTH), BF16), jax.ShapeDtypeStruct((t, c_width), F32)],
        scratch_shapes=[pltpu.VMEM((tm, 2 * A_WIDTH), F32), pltpu.VMEM((tm, 2 * A_WIDTH), F32),
                        pltpu.VMEM((tm, A_WIDTH), F32)],
        compiler_params=_params("arbitrary"),
        name="in_proj_gmlp",
    )(*xs, w_in, vg, ws, bs, oga, cds)


DFT_HALO = 16


def _dft_body(c_ref, sn_ref, xc_ref, xs_ref, og_ref, rev_ref, rev0_ref, o_ref, xcf_scr, xsf_scr, *, scale):
    tk = o_ref.shape[1]
    rows = c_ref.shape[0]
    half = xcf_scr.shape[0]
    n = xc_ref.shape[0]

    @pl.when(pl.program_id(1) == 0)
    def _():
        for m in range(half // tk):
            lo = m * tk
            start, sel = (n - rows, rev0_ref) if m == 0 else (n - lo - tk, rev_ref)
            pc = _dot(sel[...], xc_ref[start:start + rows, :])
            ps = _dot(sel[...], xs_ref[start:start + rows, :])
            xcf_scr[lo:lo + tk, :] = (xc_ref[lo:lo + tk, :].astype(F32) + pc).astype(BF16)
            xsf_scr[lo:lo + tk, :] = (xs_ref[lo:lo + tk, :].astype(F32) - ps).astype(BF16)

    sign = 1.0 - 2.0 * (lax.broadcasted_iota(jnp.int32, (rows, 1), 0) % 2).astype(F32)
    a = _dot(c_ref[...], xcf_scr[...]) + sign * xc_ref[half:half + 1, :].astype(F32)
    bn = _dot(sn_ref[...], xsf_scr[...])
    o_ref[0] = _rms((a + bn)[:tk] * scale, og_ref[...]).astype(BF16)
    mirrored = _rms((a - bn) * scale, og_ref[...]).astype(BF16)
    o_ref[1] = _dot(rev_ref[...], mirrored).astype(BF16)


def _dft_call(c_half, sn_half, rev, rev0, xc, xs, ogb, seq):
    t, w = xc.shape
    nb = t // seq
    tk, rows = rev.shape
    half = seq // 2
    nk = half // tk
    assert nk * tk == half and tk % 2 == 0 and c_half.shape[0] >= (nk - 1) * tk + rows
    scale = 1.0 / math.sqrt(seq * B_HEAD_DIM)
    lhs = pl.BlockSpec((pl.Element(rows), pl.Element(half)), lambda b, k: (k * tk, 0))
    rhs = pl.BlockSpec((seq, w), lambda b, k: (b, 0))
    return pl.pallas_call(
        functools.partial(_dft_body, scale=scale),
        grid=(nb, nk),
        in_specs=[lhs, lhs, rhs, rhs, pl.BlockSpec((1, w), lambda b, k: (0, 0)), _const_spec(rev.shape),
                  _const_spec(rev0.shape)],
        out_specs=pl.BlockSpec((None, 2, None, tk, w), lambda b, k: (b, 0, k, 0, 0)),
        out_shape=jax.ShapeDtypeStruct((nb, 2, nk, tk, w), BF16),
        scratch_shapes=[pltpu.VMEM((half, w), BF16), pltpu.VMEM((half, w), BF16)],
        compiler_params=_params("parallel", "arbitrary"),
        name="fnet_seq_dft",
    )(c_half, sn_half, xc, xs, ogb, rev, rev0)


def _dft_half_matrices(n, tk):
    nk = n // (2 * tk)
    rows = tk + DFT_HALO
    r = DFT_TABLE_SPLIT
    n_a = -(-((nk - 1) * tk + rows) // r)
    s = jnp.arange(n, dtype=jnp.int32)[None, :]
    unit = 2.0 * math.pi / n
    alpha = (((r * jnp.arange(n_a, dtype=jnp.int32)[:, None]) * s) % n).astype(F32) * unit
    beta = ((jnp.arange(r, dtype=jnp.int32)[:, None] * s) % n).astype(F32) * unit
    ca, sa, cb, sb = jnp.cos(alpha), jnp.sin(alpha), jnp.cos(beta), jnp.sin(beta)
    cos = (ca[:, None, :] * cb[None, :, :] - sa[:, None, :] * sb[None, :, :]).reshape(n_a * r, n).astype(BF16)
    nsin = (-(sa[:, None, :] * cb[None, :, :] + ca[:, None, :] * sb[None, :, :])).reshape(n_a * r, n).astype(BF16)
    col, i = jnp.arange(rows)[None, :], jnp.arange(tk)[:, None]
    return cos, nsin, (col == tk - i).astype(BF16), (col == rows - i).astype(BF16)


def _s5_weights(lam_re, lam_im, log_step, b_re, b_im, c_re, c_im, d_skip):
    L = S5_CHUNK
    G, P = lam_re.shape[1:]
    step = jnp.exp(log_step)[..., None]
    lr, li = lam_re, lam_im
    mag = jnp.exp(lr * step)
    ab_re, ab_im = mag * jnp.cos(li * step), mag * jnp.sin(li * step)
    den = lr * lr + li * li
    nr = ab_re - 1.0
    q_re = (nr * lr + ab_im * li) / den
    q_im = (ab_im * lr - nr * li) / den
    bb_re = q_re[..., None] * b_re[None] - q_im[..., None] * b_im[None]
    bb_im = q_re[..., None] * b_im[None] + q_im[..., None] * b_re[None]
    k = jnp.arange(L + 1, dtype=F32)[:, None, None, None]
    pmag = jnp.exp(k * (lr * step))
    pw_re, pw_im = pmag * jnp.cos(k * (li * step)), pmag * jnp.sin(k * (li * step))
    w_re = pw_re[..., None] * bb_re - pw_im[..., None] * bb_im
    w_im = pw_re[..., None] * bb_im + pw_im[..., None] * bb_re
    kern = (jnp.einsum('gop,ktgpc->ktgoc', c_re, w_re[:L]) - jnp.einsum('gop,ktgpc->ktgoc', c_im, w_im[:L]))
    centre = kern[0, 0] + kern[0, 1] + jnp.eye(C_GROUP) * d_skip.reshape(G, 1, C_GROUP)
    lags = jnp.concatenate([kern[1:, 1][::-1], centre[None], kern[1:, 0]], axis=0)
    lags = lags.transpose(1, 3, 0, 2).reshape(G, C_GROUP, (2 * L - 1) * C_GROUP)
    lags = jnp.pad(lags, ((0, 0), (0, 0), (0, C_GROUP)))
    to_rows = lambda a: a.transpose(1, 0, 3, 2).reshape(G, S5_COLS, P)
    p_mat = jnp.concatenate([to_rows(w_re[:L, 0][::-1]), to_rows(w_re[:L, 1]),
                             to_rows(w_im[:L, 0][::-1]), to_rows(w_im[:L, 1])], axis=-1)
    def readout(pr, pi):
        wr = c_re[None] * pr[:, :, None, :] - c_im[None] * pi[:, :, None, :]
        wi = c_re[None] * pi[:, :, None, :] + c_im[None] * pr[:, :, None, :]
        fl = lambda a: a.transpose(1, 3, 0, 2).reshape(G, P, S5_COLS)
        return fl(wr), fl(-wi)
    qf_re, qf_im = readout(pw_re[1:, 0], pw_im[1:, 0])
    qb_re, qb_im = readout(pw_re[1:, 1][::-1], pw_im[1:, 1][::-1])
    z = jnp.zeros_like(qf_re)
    qf = jnp.concatenate([qf_re, z, qf_im, z], axis=1)
    qb = jnp.concatenate([z, qb_re, z, qb_im], axis=1)
    q_mat = jnp.concatenate([qf, qb], axis=1)
    a_re = jnp.concatenate([pw_re[L, 0], pw_re[L, 1]], axis=-1)
    a_im = jnp.concatenate([pw_im[L, 0], pw_im[L, 1]], axis=-1)
    return p_mat.astype(BF16), lags, q_mat.astype(BF16), a_re, a_im


GROUPS_PER_BLOCK = LANES // C_GROUP


def _s5_body(z_ref, p_ref, lag_ref, q_ref, are_ref, aim_ref, o_ref,
             m_scr, v_scr, u_scr, s_scr, ha_scr, hb_scr, y_scr):
    nchunks = z_ref.shape[1] // S5_CHUNK
    rows = GROUPS_PER_BLOCK
    n = nchunks * rows

    @pl.when(pl.program_id(1) == 0)
    def _():
        width = lag_ref.shape[2]
        for g in range(rows):
            lag = lag_ref[g]
            for i in range(S5_CHUNK):
                start = (S5_CHUNK - 1 - i) * C_GROUP
                window = lag if start == 0 else pltpu.roll(lag, width - start, axis=1)
                m_scr[g, i * C_GROUP:(i + 1) * C_GROUP, :] = window[:, :S5_COLS].astype(BF16)

    nq = z_ref.shape[0]
    for q in range(nq):
        for i in range(S5_CHUNK):
            t = z_ref[q, pl.ds(i, nchunks, stride=S5_CHUNK), :].T
            for g in range(rows):
                v_scr[g, i * C_GROUP:(i + 1) * C_GROUP, :] = t[g * C_GROUP:(g + 1) * C_GROUP, :]
        for g in range(rows):
            u_scr[g, q * nchunks:(q + 1) * nchunks, :] = v_scr[g].T.astype(BF16)
    for g in range(rows):
        s = _dot(u_scr[g], p_ref[g])
        for q in range(nq):
            for part in range(2):
                s_scr[2 * q + part, pl.ds(g, nchunks, stride=rows), :] = (
                    s[q * nchunks:(q + 1) * nchunks, part * LANES:(part + 1) * LANES])

    ar = are_ref[...]
    ai = aim_ref[...]
    is_fwd = lax.broadcasted_iota(jnp.int32, (rows, LANES), 1) < C_STATE
    zero = jnp.zeros((rows, LANES), F32)
    for k in range(2 * nq):
        ha_scr[k, 0:rows, :] = zero
        hb_scr[k, n - rows:n, :] = zero

    def step(c, carry):
        r_f = pl.multiple_of(c * rows, rows)
        r_b = pl.multiple_of((nchunks - 1 - c) * rows, rows)
        out = []
        for q in range(nq):
            hre, him = carry[2 * q], carry[2 * q + 1]
            sre = jnp.where(is_fwd, s_scr[2 * q, pl.ds(r_f, rows), :], s_scr[2 * q, pl.ds(r_b, rows), :])
            sim = jnp.where(is_fwd, s_scr[2 * q + 1, pl.ds(r_f, rows), :], s_scr[2 * q + 1, pl.ds(r_b, rows), :])
            nre = hre * ar - him * ai + sre
            nim = hre * ai + him * ar + sim
            for part, h in enumerate((nre, nim)):
                ha_scr[2 * q + part, pl.ds(pl.multiple_of(r_f + rows, rows), rows), :] = h
                hb_scr[2 * q + part, pl.ds(pl.multiple_of(r_b - rows, rows), rows), :] = h
            out += [nre, nim]
        return tuple(out)

    lax.fori_loop(0, nchunks - 1, step, (zero,) * (2 * nq), unroll=15)

    def group_rows(scr, g):
        return jnp.concatenate(
            [jnp.concatenate([scr[2 * q + part, pl.ds(g, nchunks, stride=rows), :] for part in range(2)], axis=1)
             for q in range(nq)], axis=0).astype(BF16)

    for g in range(rows):
        ha = group_rows(ha_scr, g)
        hb = group_rows(hb_scr, g)
        y = (_dot(u_scr[g], m_scr[g]) + _dot(ha, q_ref[g, 0:S5_COLS, :])
             + _dot(hb, q_ref[g, S5_COLS:2 * S5_COLS, :]))
        for q in range(nq):
            t = y[q * nchunks:(q + 1) * nchunks, :].T
            for j in range(S5_CHUNK):
                y_scr[q * S5_CHUNK + j, g * C_GROUP:(g + 1) * C_GROUP, :] = t[j * C_GROUP:(j + 1) * C_GROUP, :]
    for q in range(nq):
        for j in range(S5_CHUNK):
            o_ref[q, pl.ds(j, nchunks, stride=S5_CHUNK), :] = y_scr[q * S5_CHUNK + j].T


def _s5_mixer(zc, nb, seq, weights, layer):
    p_mat, lags, q_mat, a_re, a_im = weights
    width = zc.shape[1]
    nchunks = seq // S5_CHUNK
    gb = GROUPS_PER_BLOCK
    per_block = lambda a: pl.BlockSpec((None, gb) + a.shape[2:], lambda b, n: (layer, b) + (0,) * (a.ndim - 2))
    nq = 2 if nb % 2 == 0 else 1
    seq_block = pl.BlockSpec((nq, seq, LANES), lambda b, n: (n, 0, b))
    state = pltpu.VMEM((2 * nq, nchunks * gb, LANES), F32)
    return pl.pallas_call(
        _s5_body,
        grid=(width // LANES, nb // nq),
        in_specs=[seq_block, per_block(p_mat), per_block(lags), per_block(q_mat), per_block(a_re),
                  per_block(a_im)],
        out_specs=seq_block,
        out_shape=jax.ShapeDtypeStruct((nb, seq, width), F32),
        scratch_shapes=[pltpu.VMEM((gb, S5_COLS, S5_COLS), BF16),
                        pltpu.VMEM((gb, S5_COLS, nchunks), F32), pltpu.VMEM((gb, nq * nchunks, S5_COLS), BF16),
                        state, state, state, pltpu.VMEM((nq * S5_CHUNK, LANES, nchunks), F32)],
        compiler_params=_params("parallel", "arbitrary"),
        name="s5_mixer",
    )(zc.reshape(nb, seq, width), p_mat, lags, q_mat, a_re, a_im).reshape(nb * seq, width)


def _out_body(*refs, tile_bounds):
    nx = len(tile_bounds)
    x_refs = refs[:nx]
    ya_ref, yb_ref, yc_ref, wg_ref, bg_ref, ogc_ref, wo_ref, o_ref = refs[nx:]
    acc = _select_rows(x_refs, tile_bounds) + _dot(ya_ref[...], wo_ref[0:A_WIDTH, :])
    acc = acc + _dot(yb_ref[...], wo_ref[A_WIDTH:A_WIDTH + B_WIDTH, :])
    y = _gelu(yc_ref[...].astype(F32))
    yc = y * _sigmoid(_dot(y.astype(BF16), wg_ref[...]) + bg_ref[...])
    yc = _rms(yc, ogc_ref[...]).astype(BF16)
    o_ref[...] = acc + _dot(yc, wo_ref[A_WIDTH + B_WIDTH:, :])


def _out_call(xs, ya, yb, yc, w_glu, b_glu, ogc, w_out, layer):
    t, cw = yc.shape
    d = w_out.shape[2]
    tm = TOKEN_TILE
    x_specs, tile_bounds = _row_tile_specs(xs, tm)
    assert tile_bounds[-1] * tm == t
    row = lambda w: pl.BlockSpec((tm, w), lambda i: (i, 0))
    _, _, nk, tk, bw = yb.shape
    assert tk == tm

    def yb_index(i):
        j = i % (2 * nk)
        return (i // (2 * nk), j // nk, jnp.where(j < nk, j, 2 * nk - 1 - j), 0, 0)

    return pl.pallas_call(
        functools.partial(_out_body, tile_bounds=tile_bounds),
        grid=(t // tm,),
        in_specs=x_specs + [row(A_WIDTH), pl.BlockSpec((None, None, None, tm, bw), yb_index), row(cw),
                            _const_spec((cw, cw), layer),
                            _const_spec((1, cw)), _const_spec((1, cw)), _const_spec(w_out.shape[1:], layer)],
        out_specs=row(d),
        out_shape=jax.ShapeDtypeStruct((t, d), F32),
        compiler_params=_params("parallel"),
        name="glu_out_proj",
    )(*xs, ya, yb, yc, w_glu, b_glu, ogc, w_out)


def _ffn_body(x_ref, wg_ref, wu_ref, wd_ref, gf_ref, o_ref, h_scr, inv_scr, *, final_norm):
    f = pl.program_id(1)

    def ff_tile(h, inv):
        gate = _dot(h, wg_ref[...]) * inv
        a = (gate * _sigmoid(gate) * (_dot(h, wu_ref[...]) * inv)).astype(BF16)
        return _dot(a, wd_ref[...])

    @pl.when(f == 0)
    def _():
        x = x_ref[...]
        h = x.astype(BF16)
        inv = lax.rsqrt(jnp.mean(x * x, axis=-1, keepdims=True) + EPS)
        h_scr[...] = h
        inv_scr[...] = jnp.broadcast_to(inv, inv_scr.shape)
        o_ref[...] = x + ff_tile(h, inv)

    @pl.when(f > 0)
    def _():
        o_ref[...] += ff_tile(h_scr[...], inv_scr[:, 0:1])

    if final_norm:
        @pl.when(f == pl.num_programs(1) - 1)
        def _():
            o_ref[...] = _rms(o_ref[...], gf_ref[...])


def _ffn_call(x, w_gate, w_up, w_down, layer, gf, final_norm, row_start=0, n_rows=None):
    d = x.shape[1]
    ff = w_gate.shape[2]
    n_rows = x.shape[0] if n_rows is None else n_rows
    tm = min(FFN_TOKEN_TILE, n_rows)
    tf = FF_TILE
    assert n_rows % tm == 0 and row_start % tm == 0 and ff % tf == 0
    first = row_start // tm
    return pl.pallas_call(
        functools.partial(_ffn_body, final_norm=final_norm),
        grid=(n_rows // tm, ff // tf),
        in_specs=[pl.BlockSpec((tm, d), lambda i, f: (i + first, 0)),
                  pl.BlockSpec((None, d, tf), lambda i, f: (layer, 0, f)),
                  pl.BlockSpec((None, d, tf), lambda i, f: (layer, 0, f)),
                  pl.BlockSpec((None, tf, d), lambda i, f: (layer, f, 0)), _const_spec((1, d))],
        out_specs=pl.BlockSpec((tm, d), lambda i, f: (i, 0)),
        out_shape=jax.ShapeDtypeStruct((n_rows, d), F32),
        scratch_shapes=[pltpu.VMEM((tm, d), BF16), pltpu.VMEM((tm, LANES), F32)],
        compiler_params=_params("parallel", "arbitrary"),
        name="swiglu_ffn",
    )(x, w_gate, w_up, w_down, gf)


def _dft_matrices(n):
    k = jnp.arange(n, dtype=jnp.int32)
    ang = ((k[:, None] * k[None, :]) % n).astype(F32) * (2.0 * math.pi / n)
    return jnp.cos(ang), jnp.sin(ang)


def kernel(x_prompt, x_sample, norm1_g, w_in, a_v_g, a_ws, a_bs, c_lam_re, c_lam_im, c_log_step, c_b_re, c_b_im,
           c_c_re, c_c_im, c_d, c_w_glu, c_b_glu, out_norm_g, w_out, norm2_g, w_gate, w_up, w_down, final_g):
    seq, d = x_prompt.shape[1:]
    assert x_sample.shape[1:] == (seq, d)
    nb = x_prompt.shape[0] + x_sample.shape[0]
    n_prompt = x_prompt.shape[0] * seq
    depth = w_in.shape[0]
    xs = (x_prompt.reshape(-1, d), x_sample.reshape(-1, d))

    c_half, sn_half, rev, rev0 = _dft_half_matrices(seq, min(DFT_ROW_TILE, seq // 2))
    cds = jnp.concatenate(_dft_matrices(B_HEAD_DIM), axis=1).astype(BF16)
    row = lambda v: v.reshape(1, -1)
    g1, g2 = norm1_g[:, :, None], norm2_g[:, :, None]
    w_in, w_out, c_w_glu, w_gate, w_up, w_down = (
        w.astype(BF16) for w in (w_in * g1, w_out, c_w_glu, w_gate * g2, w_up * g2, w_down))
    s5w = jax.vmap(_s5_weights)(c_lam_re, c_lam_im, c_log_step, c_b_re, c_b_im, c_c_re, c_c_im, c_d)

    ws = a_ws.astype(BF16)
    bs = jnp.broadcast_to(a_bs[..., None], a_bs.shape + (A_HEAD_DIM,))

    for l in range(depth):
        og = out_norm_g[l]
        ya, xc, xsin, zc = _in_call(xs, w_in, l, row(a_v_g[l]), ws, bs, row(og[:A_WIDTH]), cds)
        yb = _dft_call(c_half, sn_half, rev, rev0, xc, xsin, row(og[A_WIDTH:A_WIDTH + B_WIDTH]), seq)
        yc = _s5_mixer(zc, nb, seq, s5w, l)
        x = _out_call(xs, ya, yb, yc, c_w_glu, row(c_b_glu[l]), row(og[A_WIDTH + B_WIDTH:]), w_out, l)
        ffn = functools.partial(_ffn_call, x, w_gate, w_up, w_down, l, row(final_g))
        if l < depth - 1:
            xs = (ffn(final_norm=False),)
        else:
            y_prompt = ffn(final_norm=True, row_start=0, n_rows=n_prompt)
            y_sample = ffn(final_norm=True, row_start=n_prompt, n_rows=x.shape[0] - n_prompt)

    return (y_prompt.reshape(x_prompt.shape), y_sample.reshape(x_sample.shape))
```
